```python
import math
import jax
import jax.numpy as jnp
from jax import lax
import numpy as np

D_MODEL = 1024
BATCH = 16
SEQ = 2048
DEPTH = 2
DEC_BATCH = 32
DEC_SEQ = 1
PAST_LEN = 16384
PAGE_SIZE = 128

HA = 4
DA = 64
DVA = 2 * DA
HB = 4
DB = 128
MOBA_BLOCK = 256
MOBA_TOPK = 3
MOBA_Q_CHUNK = 8
ATTN_Q_BLOCK = 128
N_BUCKETS = 32
MAX_DISTANCE = 128
HC = 4
DKC = 128
DVC = 128
HGRN_CHUNK = 32
HD = 8
DD = 64
D_DECAY_LORA = 32
D_AAA_LORA = 32
D_GATE_LORA = 96
RWKV_LN_EPS = 64e-5
D_FF = 2816
N_EXPERTS = 8
TOP_K = 2
D_FF_EXPERT = 3584
EPS = 1e-6

A_QK = HA * 2 * DA
A_V = HA * DVA
B_W = HB * DB
EVEN_SPLITS = (A_QK, A_QK, A_V, B_W, B_W, B_W)
EVEN_IN = A_QK * 2 + A_V + B_W * 3
EVEN_OUT = A_V + B_W
C_K = HC * DKC
C_V = HC * DVC
HGRN_SPLITS = (C_K, C_K, C_V, C_V)
HGRN_IN = 2 * C_K + 2 * C_V
D_W = HD * DD
RWKV_SPLITS = (D_W, D_DECAY_LORA, D_W, D_W, D_AAA_LORA, D_GATE_LORA)
RWKV_IN = 3 * D_W + D_DECAY_LORA + D_AAA_LORA + D_GATE_LORA
ODD_IN = HGRN_IN + RWKV_IN
ODD_OUT = C_V + D_W

kernel_name = 'hybrid_diffattn_moba_hgrn2_rwkv7_step'


def _split(z, sizes):
    offs = np.cumsum(sizes)[:-1].tolist()
    return jnp.split(z, offs, axis=-1)


def _rmsnorm(x, g):
    xf = x.astype(jnp.float32)
    y = xf * lax.rsqrt(jnp.mean(xf * xf, axis=-1, keepdims=True) + EPS)
    return (y * g.astype(jnp.float32)).astype(x.dtype)


def _t5_bucket(rel):
    max_exact = N_BUCKETS // 2
    n = jnp.maximum(rel, 0)
    nf = jnp.maximum(n, max_exact).astype(jnp.float32)
    large = max_exact + (jnp.log(nf / max_exact) / math.log(MAX_DISTANCE / max_exact)
                         * (N_BUCKETS - max_exact)).astype(jnp.int32)
    return jnp.where(n < max_exact, n, jnp.minimum(large, N_BUCKETS - 1))


def _diff_attention(q, k_new, v_new, q_start, lam, bias_tab, past):
    B, Tq = q.shape[0], q.shape[1]
    Tn = k_new.shape[1]
    qc = min(ATTN_Q_BLOCK, Tq)
    nblk = -(-Tq // qc)
    qp = jnp.pad(q, ((0, 0), (0, nblk * qc - Tq), (0, 0), (0, 0), (0, 0)))
    qb = jnp.moveaxis(qp.reshape(B, nblk, qc, HA, 2, DA), 1, 0)
    scale = DA ** -0.5
    tab = bias_tab.astype(jnp.float32).T
    new_pos = q_start + jnp.arange(Tn)

    def block(args):
        qi, bi = args
        qpos = q_start + bi * qc + jnp.arange(qc)
        logits = jnp.einsum('bqhcd,bkhcd->bchqk', qi, k_new).astype(jnp.float32) * scale
        logits = logits + tab[:, _t5_bucket(qpos[:, None] - new_pos[None, :])]
        logits = jnp.where(new_pos[None, :] <= qpos[:, None], logits, -jnp.inf)
        if past is not None:
            k_past = past[0]
            past_pos = jnp.arange(k_past.shape[1])
            lp = jnp.einsum('bqhcd,bkhcd->bchqk', qi, k_past).astype(jnp.float32) * scale
            lp = lp + tab[:, _t5_bucket(qpos[:, None] - past_pos[None, :])]
            logits = jnp.concatenate([lp, logits], axis=-1)
        p = jax.nn.softmax(logits, axis=-1)
        attn = (p[:, 0] - lam * p[:, 1]).astype(v_new.dtype)
        out = jnp.einsum('bhqk,bkhd->bqhd', attn[..., logits.shape[-1] - Tn:], v_new)
        if past is not None:
            out = out + jnp.einsum('bhqk,bkhd->bqhd', attn[..., :past[0].shape[1]], past[1])
        return out

    out = lax.map(block, (qb, jnp.arange(nblk)))
    return jnp.moveaxis(out, 0, 1).reshape(B, nblk * qc, HA, DVA)[:, :Tq]


def _moba_attention(q, q_start, block_means, fetch, bias_tab):
    B, Tq = q.shape[0], q.shape[1]
    nb = block_means.shape[1]
    if nb < MOBA_TOPK:
        block_means = jnp.pad(block_means, ((0, 0), (0, MOBA_TOPK - nb), (0, 0), (0, 0)))
        nb = MOBA_TOPK
    qc = min(MOBA_Q_CHUNK, Tq)
    nchunk = -(-Tq // qc)
    qp = jnp.pad(q, ((0, 0), (0, nchunk * qc - Tq), (0, 0), (0, 0)))
    qchunks = jnp.moveaxis(qp.reshape(B, nchunk, qc, HB, DB), 1, 0)
    scale = DB ** -0.5
    tab = bias_tab.astype(jnp.float32).T
    head = jnp.arange(HB)[None, :, None, None]
    blk_idx = jnp.arange(nb)
    slot = jnp.arange(MOBA_TOPK)
    offs = jnp.arange(MOBA_BLOCK)
    n_keys = (MOBA_TOPK + 1) * MOBA_BLOCK
    means = block_means.astype(jnp.float32)

    def chunk(args):
        qi, ci = args
        qpos = q_start + ci * qc + jnp.arange(qc)
        own = qpos // MOBA_BLOCK
        gate = jnp.einsum('bqhd,bnhd->bhqn', qi.astype(jnp.float32), means)
        gate = jnp.where(blk_idx[None, None, None, :] < own[None, None, :, None], gate, -jnp.inf)
        _, sel = lax.top_k(gate, MOBA_TOPK)
        blocks = jnp.concatenate([sel, jnp.broadcast_to(own[None, None, :, None], (B, HB, qc, 1))], axis=-1)
        pos = (blocks[..., None] * MOBA_BLOCK + offs).reshape(B, HB, qc, n_keys)
        sel_ok = jnp.broadcast_to((slot[None, :] < own[:, None])[:, :, None], (qc, MOBA_TOPK, MOBA_BLOCK))
        own_ok = (own[:, None] * MOBA_BLOCK + offs[None, :] <= qpos[:, None])[:, None, :]
        ok = jnp.concatenate([sel_ok, own_ok], axis=1).reshape(qc, n_keys)
        k, v = fetch(pos)
        logits = jnp.einsum('bqhd,bhqsd->bhqs', qi, k).astype(jnp.float32) * scale
        logits = logits + tab[head, _t5_bucket(qpos[None, None, :, None] - pos)]
        logits = jnp.where(ok[None, None], logits, -jnp.inf)
        p = jax.nn.softmax(logits, axis=-1).astype(v.dtype)
        return jnp.einsum('bhqs,bhqsd->bqhd', p, v)

    out = lax.map(chunk, (qchunks, jnp.arange(nchunk)))
    return jnp.moveaxis(out, 0, 1).reshape(B, nchunk * qc, HB, DB)[:, :Tq]


def _even_mixer(hn, q_start, paged, w_in, lam_q1, lam_k1, lam_q2, lam_k2, subln_g, rel_bias, w_out, layer):
    B, T, _ = hn.shape
    qa, ka, va, qb, kb, vb = _split(hn @ w_in, EVEN_SPLITS)
    qa = qa.reshape(B, T, HA, 2, DA)
    ka = ka.reshape(B, T, HA, 2, DA)
    va = va.reshape(B, T, HA, DVA)
    qb = qb.reshape(B, T, HB, DB)
    kb = kb.reshape(B, T, HB, DB)
    vb = vb.reshape(B, T, HB, DB)
    lam_init = 0.8 - 0.6 * math.exp(-0.3 * layer)
    f32 = jnp.float32
    lam = (jnp.exp(jnp.sum(lam_q1.astype(f32) * lam_k1.astype(f32)))
           - jnp.exp(jnp.sum(lam_q2.astype(f32) * lam_k2.astype(f32))) + lam_init)
    bidx = jnp.arange(B)[:, None, None, None]
    hidx = jnp.arange(HB)[None, :, None, None]
    if paged is None:
        past_a = None
        nb = -(-T // MOBA_BLOCK)
        means = jnp.zeros((B, nb, HB, DB), f32).at[:, jnp.arange(T) // MOBA_BLOCK].add(kb.astype(f32)) / MOBA_BLOCK

        def fetch(pos):
            pc = jnp.minimum(pos, T - 1)
            return kb[bidx, pc, hidx], vb[bidx, pc, hidx]
    else:
        page_table, ckd, cvd, ckm, cvm = paged
        n_pages = page_table.shape[1]
        P = n_pages * PAGE_SIZE
        past_a = (ckd[page_table].reshape(B, P, HA, 2, DA).astype(ka.dtype),
                  cvd[page_table].reshape(B, P, HA, DVA).astype(va.dtype))
        page_sums = jnp.sum(ckm[page_table].astype(f32), axis=2)
        nb = -(-(P + T) // MOBA_BLOCK)
        means = (jnp.zeros((B, nb, HB, DB), f32)
                 .at[:, (jnp.arange(n_pages) * PAGE_SIZE) // MOBA_BLOCK].add(page_sums)
                 .at[:, (P + jnp.arange(T)) // MOBA_BLOCK].add(kb.astype(f32))) / MOBA_BLOCK

        def fetch(pos):
            pp = jnp.minimum(pos, P - 1)
            page = page_table[bidx, pp // PAGE_SIZE]
            off = pp % PAGE_SIZE
            pn = jnp.clip(pos - P, 0, T - 1)
            in_past = (pos < P)[..., None]
            k = jnp.where(in_past, ckm[page, off, hidx].astype(kb.dtype), kb[bidx, pn, hidx])
            v = jnp.where(in_past, cvm[page, off, hidx].astype(vb.dtype), vb[bidx, pn, hidx])
            return k, v
    oa = _diff_attention(qa, ka, va, q_start, lam, rel_bias[:, :HA], past_a)
    oa = _rmsnorm(oa, subln_g) * (1.0 - lam_init)
    ob = _moba_attention(qb, q_start, means, fetch, rel_bias[:, HA:])
    y = jnp.concatenate([oa.reshape(B, T, A_V), ob.reshape(B, T, B_W).astype(oa.dtype)], axis=-1) @ w_out
    return y, (ka, va, kb, vb)


def _hgrn2_chunked(q, k, logf, i, s0):
    B, T, H, _ = q.shape
    DV = i.shape[-1]
    c = min(HGRN_CHUNK, T)
    n = -(-T // c)
    pad = ((0, 0), (0, n * c - T), (0, 0), (0, 0))

    def chunks(t):
        t = jnp.pad(t, pad)
        return t.reshape(B, n, c, H, t.shape[-1]).transpose(1, 0, 3, 2, 4)

    q, k, logf, i = chunks(q), chunks(k), chunks(logf), chunks(i)
    b = jnp.cumsum(logf, axis=3)
    b_ref = b[:, :, :, c // 2:c // 2 + 1]
    b_last = b[:, :, :, -1:]
    causal = jnp.tril(jnp.ones((c, c), dtype=bool))
    scores = jnp.einsum('nbhtd,nbhsd->nbhts', q * jnp.exp(b - b_ref), k * jnp.exp(b_ref - b))
    o_intra = jnp.einsum('nbhts,nbhsv->nbhtv', jnp.where(causal, scores, 0.0), i)
    q_inter = q * jnp.exp(b)
    k_state = k * jnp.exp(b_last - b)
    chunk_decay = jnp.exp(b_last[:, :, :, 0])

    def step(S, xs):
        qx, kx, ix, dx = xs
        o = jnp.einsum('bhtd,bhdv->bhtv', qx, S)
        S = S * dx[..., None] + jnp.einsum('bhtd,bhtv->bhdv', kx, ix)
        return S, o

    s_T, o_inter = lax.scan(step, s0, (q_inter, k_state, i, chunk_decay))
    o = (o_inter + o_intra).transpose(1, 0, 3, 2, 4).reshape(B, n * c, H, DV)[:, :T]
    return o, s_T


def _rwkv7_scan(r, w, k, v, kk, a, s0):
    xs = tuple(jnp.moveaxis(t, 1, 0) for t in (r, w, k, v, kk, a))

    def step(S, xt):
        rt, wt, kt, vt, kkt, at = xt
        sa = jnp.einsum('bhvk,bhk->bhv', S, -kkt)
        S = (S * wt[:, :, None, :] + sa[..., None] * (kkt * at)[:, :, None, :]
             + vt[..., None] * kt[:, :, None, :])
        return S, jnp.einsum('bhvk,bhk->bhv', S, rt)

    s_T, y = lax.scan(step, s0, xs)
    return jnp.moveaxis(y, 0, 1), s_T


def _odd_mixer(hn, x_prev, s_hgrn, s_rwkv, layer, w_in, hgrn_lb, hgrn_norm_g, rwkv_mu, rwkv_w0, rwkv_w2,
               rwkv_a0, rwkv_a2, rwkv_g2, rwkv_k_k, rwkv_k_a, rwkv_r_k, rwkv_ln_w, rwkv_ln_b, w_out):
    B, T, _ = hn.shape
    f32 = jnp.float32
    z_all = jnp.concatenate([x_prev[:, None, :].astype(hn.dtype), hn], axis=1) @ w_in
    zc = z_all[:, 1:, :HGRN_IN].astype(f32)
    zr_cur = z_all[:, 1:, HGRN_IN:]
    zr_prev = z_all[:, :-1, HGRN_IN:]
    zr = (zr_cur + (zr_prev - zr_cur) * rwkv_mu).astype(f32)
    cq, cf, ci, cg = _split(zc, HGRN_SPLITS)
    lb_w = jax.nn.softmax(hgrn_lb.astype(f32), axis=0)
    lb = (jnp.cumsum(lb_w, axis=0) - lb_w[0])[layer]
    f = lb + (1.0 - lb) * jax.nn.sigmoid(cf)
    o_c, s_hgrn_new = _hgrn2_chunked(jax.nn.silu(cq).reshape(B, T, HC, DKC),
                                     (1.0 - f).reshape(B, T, HC, DKC),
                                     jnp.log(f).reshape(B, T, HC, DKC),
                                     ci.reshape(B, T, HC, DVC), s_hgrn.astype(f32))
    o_c = _rmsnorm(o_c, hgrn_norm_g).reshape(B, T, C_V) * jax.nn.silu(cg)
    r, wd, k, v, ad, gd = _split(zr, RWKV_SPLITS)
    w = -jax.nn.softplus(-(rwkv_w0.astype(f32) + jnp.tanh(wd) @ rwkv_w2.astype(f32))) - 0.5
    decay = jnp.exp(-jnp.exp(w))
    a = jax.nn.sigmoid(rwkv_a0.astype(f32) + ad @ rwkv_a2.astype(f32))
    g = jax.nn.sigmoid(gd) @ rwkv_g2.astype(f32)
    kk = (k * rwkv_k_k.astype(f32)).reshape(B, T, HD, DD)
    kk = kk / jnp.maximum(jnp.sqrt(jnp.sum(kk * kk, axis=-1, keepdims=True)), 1e-12)
    k = k * (1.0 + (a - 1.0) * rwkv_k_a.astype(f32))
    r, decay, k, v, a = (t.reshape(B, T, HD, DD) for t in (r, decay, k, v, a))
    y, s_rwkv_new = _rwkv7_scan(r, decay, k, v, kk, a, s_rwkv.astype(f32))
    mu = jnp.mean(y, axis=-1, keepdims=True)
    var = jnp.mean(jnp.square(y - mu), axis=-1, keepdims=True)
    y = ((y - mu) * lax.rsqrt(var + RWKV_LN_EPS)).reshape(B, T, D_W) * rwkv_ln_w.astype(f32) + rwkv_ln_b.astype(f32)
    y = y + (jnp.sum(r * k * rwkv_r_k.astype(f32), axis=-1, keepdims=True) * v).reshape(B, T, D_W)
    o_d = y * g
    out = jnp.concatenate([o_c, o_d], axis=-1).astype(hn.dtype) @ w_out
    return out, (s_hgrn_new.astype(s_hgrn.dtype), s_rwkv_new.astype(s_rwkv.dtype), hn[:, -1].astype(x_prev.dtype))


def _swiglu(x, w_gate, w_up, w_down):
    return (jax.nn.silu(x @ w_gate) * (x @ w_up)) @ w_down


def _moe(x, w_router, w_gate, w_up, w_down):
    logits = (x @ w_router).astype(jnp.float32)
    top_val, top_idx = lax.top_k(logits, TOP_K)
    weights = jax.nn.softmax(top_val, axis=-1)
    gate = jnp.sum(jax.nn.one_hot(top_idx, N_EXPERTS, dtype=jnp.float32) * weights[..., None], axis=-2).astype(x.dtype)
    out = jnp.zeros_like(x)
    for e in range(N_EXPERTS):
        h = jax.nn.silu(x @ w_gate[e]) * (x @ w_up[e])
        out = out + (h @ w_down[e]) * gate[..., e:e + 1]
    return out


def setup_inputs(seed: int = 0) -> dict:
    key = jax.random.key(seed)
    ks = iter(jax.random.split(key, 80))
    f32 = jnp.float32

    def nrm(shape, scale):
        return jax.random.normal(next(ks), shape, f32) * scale

    def gain(n):
        return 1.0 + nrm((n,), 0.05)

    n_pages = PAST_LEN // PAGE_SIZE
    n_phys = (DEC_BATCH * n_pages * 5) // 4
    page_table = jax.random.permutation(next(ks), n_phys)[:DEC_BATCH * n_pages].reshape(DEC_BATCH, n_pages).astype(jnp.int32)
    return {
        'x_prompt': nrm((BATCH, SEQ, D_MODEL), 1.0),
        'x_sample': nrm((DEC_BATCH, DEC_SEQ, D_MODEL), 1.0),
        'cache_k_diff': nrm((n_phys, PAGE_SIZE, HA, 2, DA), 1.0),
        'cache_v_diff': nrm((n_phys, PAGE_SIZE, HA, DVA), 1.0),
        'cache_k_moba': nrm((n_phys, PAGE_SIZE, HB, DB), 1.0),
        'cache_v_moba': nrm((n_phys, PAGE_SIZE, HB, DB), 1.0),
        'state_hgrn': nrm((DEC_BATCH, HC, DKC, DVC), 0.5),
        'state_rwkv': nrm((DEC_BATCH, HD, DD, DD), 0.5),
        'state_shift': nrm((DEC_BATCH, D_MODEL), 1.0),
        'page_table': page_table,
        'ln_mix_0': gain(D_MODEL),
        'w_in_0': nrm((D_MODEL, EVEN_IN), D_MODEL ** -0.5),
        'lambda_q1': nrm((DA,), 0.1),
        'lambda_k1': nrm((DA,), 0.1),
        'lambda_q2': nrm((DA,), 0.1),
        'lambda_k2': nrm((DA,), 0.1),
        'subln_g': gain(DVA),
        'rel_bias': nrm((N_BUCKETS, HA + HB), 0.5),
        'w_out_0': nrm((EVEN_OUT, D_MODEL), EVEN_OUT ** -0.5),
        'ln_ffn_0': gain(D_MODEL),
        'ffn_w_gate': nrm((D_MODEL, D_FF), D_MODEL ** -0.5),
        'ffn_w_up': nrm((D_MODEL, D_FF), D_MODEL ** -0.5),
        'ffn_w_down': nrm((D_FF, D_MODEL), D_FF ** -0.5),
        'ln_mix_1': gain(D_MODEL),
        'w_in_1': nrm((D_MODEL, ODD_IN), D_MODEL ** -0.5),
        'hgrn_lb': nrm((DEPTH, C_K), 0.1),
        'hgrn_norm_g': gain(DVC),
        'rwkv_mu': jax.random.uniform(next(ks), (RWKV_IN,), f32),
        'rwkv_w0': nrm((D_W,), 0.5) - 1.0,
        'rwkv_w2': nrm((D_DECAY_LORA, D_W), 0.1),
        'rwkv_a0': nrm((D_W,), 0.1),
        'rwkv_a2': nrm((D_AAA_LORA, D_W), 0.1),
        'rwkv_g2': nrm((D_GATE_LORA, D_W), D_GATE_LORA ** -0.5),
        'rwkv_k_k': 0.85 + nrm((D_W,), 0.05),
        'rwkv_k_a': gain(D_W),
        'rwkv_r_k': nrm((HD, DD), 0.1),
        'rwkv_ln_w': gain(D_W),
        'rwkv_ln_b': nrm((D_W,), 0.02),
        'w_out_1': nrm((ODD_OUT, D_MODEL), ODD_OUT ** -0.5),
        'ln_ffn_1': gain(D_MODEL),
        'moe_router': nrm((D_MODEL, N_EXPERTS), D_MODEL ** -0.5),
        'moe_w_gate': nrm((N_EXPERTS, D_MODEL, D_FF_EXPERT), D_MODEL ** -0.5),
        'moe_w_up': nrm((N_EXPERTS, D_MODEL, D_FF_EXPERT), D_MODEL ** -0.5),
        'moe_w_down': nrm((N_EXPERTS, D_FF_EXPERT, D_MODEL), D_FF_EXPERT ** -0.5),
        'ln_final': gain(D_MODEL),
    }


def reference(x_prompt, x_sample, cache_k_diff, cache_v_diff, cache_k_moba, cache_v_moba, state_hgrn, state_rwkv,
              state_shift, page_table, ln_mix_0, w_in_0, lambda_q1, lambda_k1, lambda_q2, lambda_k2, subln_g,
              rel_bias, w_out_0, ln_ffn_0, ffn_w_gate, ffn_w_up, ffn_w_down, ln_mix_1, w_in_1, hgrn_lb,
              hgrn_norm_g, rwkv_mu, rwkv_w0, rwkv_w2, rwkv_a0, rwkv_a2, rwkv_g2, rwkv_k_k, rwkv_k_a, rwkv_r_k,
              rwkv_ln_w, rwkv_ln_b, w_out_1, ln_ffn_1, moe_router, moe_w_gate, moe_w_up, moe_w_down, ln_final):

    def trunk(x, q_start, paged, s_hgrn, s_rwkv, x_prev):
        h = x
        kv_rows = None
        rec = None
        for layer in range(DEPTH):
            if layer % 2 == 0:
                y, kv_rows = _even_mixer(_rmsnorm(h, ln_mix_0), q_start, paged, w_in_0, lambda_q1, lambda_k1,
                                         lambda_q2, lambda_k2, subln_g, rel_bias, w_out_0, layer)
                h = h + y
                h = h + _swiglu(_rmsnorm(h, ln_ffn_0), ffn_w_gate, ffn_w_up, ffn_w_down)
            else:
                y, rec = _odd_mixer(_rmsnorm(h, ln_mix_1), x_prev, s_hgrn, s_rwkv, layer, w_in_1, hgrn_lb,
                                    hgrn_norm_g, rwkv_mu, rwkv_w0, rwkv_w2, rwkv_a0, rwkv_a2, rwkv_g2, rwkv_k_k,
                                    rwkv_k_a, rwkv_r_k, rwkv_ln_w, rwkv_ln_b, w_out_1)
                h = h + y
                h = h + _moe(_rmsnorm(h, ln_ffn_1), moe_router, moe_w_gate, moe_w_up, moe_w_down)
        return _rmsnorm(h, ln_final), kv_rows, rec

    bp = x_prompt.shape[0]
    dt = x_prompt.dtype
    y_prompt, (k_diff_p, v_diff_p, k_moba_p, v_moba_p), (hgrn_p, rwkv_p, shift_p) = trunk(
        x_prompt, 0, None,
        jnp.zeros((bp, HC, DKC, DVC), dt), jnp.zeros((bp, HD, DD, DD), dt), jnp.zeros((bp, D_MODEL), dt))
    past_len = page_table.shape[1] * PAGE_SIZE
    y_sample, (k_diff_s, v_diff_s, k_moba_s, v_moba_s), (hgrn_s, rwkv_s, shift_s) = trunk(
        x_sample, past_len, (page_table, cache_k_diff, cache_v_diff, cache_k_moba, cache_v_moba),
        state_hgrn, state_rwkv, state_shift)
    return (y_prompt, y_sample, k_diff_p, v_diff_p, k_moba_p, v_moba_p, hgrn_p, rwkv_p, shift_p,
            k_diff_s, v_diff_s, k_moba_s, v_moba_s, hgrn_s, rwkv_s, shift_s)
```

```python
import functools
import math

import jax
import jax.numpy as jnp
import numpy as np
from jax import lax
from jax.experimental import pallas as pl
from jax.experimental.pallas import tpu as pltpu

F32 = jnp.float32
BF16 = jnp.bfloat16
MXU_DTYPE = jnp.bfloat16

D_MODEL = 1024
PAGE_SIZE = 128
HA, DA = 4, 64
DVA = 2 * DA
HB, DB = 4, 128
MOBA_BLOCK = 256
MOBA_TOPK = 3
N_BUCKETS = 32
MAX_DISTANCE = 128
HC, DKC, DVC = 4, 128, 128
HD, DD = 8, 64
D_DECAY_LORA, D_AAA_LORA, D_GATE_LORA = 32, 32, 96
RWKV_LN_EPS = 64e-5
N_EXPERTS = 8
EPS = 1e-6

A_QK = HA * 2 * DA
A_V = HA * DVA
B_W = HB * DB
C_K = HC * DKC
C_V = HC * DVC
HGRN_IN = 2 * C_K + 2 * C_V
D_W = HD * DD
RWKV_IN = 3 * D_W + D_DECAY_LORA + D_AAA_LORA + D_GATE_LORA

NEG = -1e30
VMEM_LIMIT_BYTES = 56 * 1024 * 1024
ATTN_TILE = 256


def _params(*sem):
    return pltpu.CompilerParams(dimension_semantics=sem, vmem_limit_bytes=VMEM_LIMIT_BYTES)


def _mm(a, b):
    return jnp.dot(a.astype(MXU_DTYPE), b.astype(MXU_DTYPE), preferred_element_type=F32)


def _mm_nt(a, b):
    return lax.dot_general(a.astype(MXU_DTYPE), b.astype(MXU_DTYPE), (((1,), (1,)), ((), ())),
                           preferred_element_type=F32)


def _rms(x, g):
    return x * lax.rsqrt(jnp.mean(x * x, axis=-1, keepdims=True) + EPS) * g


def _norm_matmul_kernel(x_ref, g_ref, w_ref, *out_refs, splits):
    y = _rms(x_ref[...], g_ref[...]).astype(MXU_DTYPE)
    off = 0
    for o_ref, width in zip(out_refs, splits):
        o_ref[...] = jnp.dot(y, w_ref[:, off:off + width], preferred_element_type=F32).astype(o_ref.dtype)
        off += width


def _norm_matmul(x, g, w, splits, dtypes, tm):
    n, d = x.shape
    tm = min(tm, n)
    assert n % tm == 0 and sum(splits) == w.shape[1]
    return pl.pallas_call(
        functools.partial(_norm_matmul_kernel, splits=tuple(splits)),
        grid=(n // tm,),
        in_specs=[pl.BlockSpec((tm, d), lambda i: (i, 0)),
                  pl.BlockSpec((1, d), lambda i: (0, 0)),
                  pl.BlockSpec(w.shape, lambda i: (0, 0))],
        out_specs=[pl.BlockSpec((tm, s), lambda i: (i, 0)) for s in splits],
        out_shape=[jax.ShapeDtypeStruct((n, s), dt) for s, dt in zip(splits, dtypes)],
        compiler_params=_params("parallel"),
        name="norm_matmul",
    )(x, g.reshape(1, d), w.astype(MXU_DTYPE))


def _t5_bucket_np(rel):
    max_exact = N_BUCKETS // 2
    n = np.maximum(rel, 0)
    nf = np.maximum(n, max_exact).astype(np.float64)
    v = np.log(nf / max_exact) / math.log(MAX_DISTANCE / max_exact) * (N_BUCKETS - max_exact)
    frac = np.abs(v - np.round(v))
    assert np.all((frac > 1e-6) | (n <= max_exact) | (v >= N_BUCKETS - max_exact - 1e-6))
    large = max_exact + np.floor(v + 1e-9).astype(np.int64)
    return np.where(n < max_exact, n, np.minimum(large, N_BUCKETS - 1)).astype(np.int32)


def _bias_tiles_kernel(tab_ref, bk_ref, o_ref):
    h = pl.program_id(0)
    bk = bk_ref[...]
    acc = jnp.zeros(bk.shape, F32)
    for i in range(N_BUCKETS):
        acc = jnp.where(bk == i, tab_ref[i, h], acc)
    o_ref[0] = acc


def _bias_tiles(rel_bias, buckets):
    nh = rel_bias.shape[1]
    return pl.pallas_call(
        _bias_tiles_kernel,
        grid=(nh,),
        in_specs=[pl.BlockSpec(memory_space=pltpu.SMEM),
                  pl.BlockSpec(buckets.shape, lambda h: (0, 0, 0))],
        out_specs=pl.BlockSpec((1,) + buckets.shape, lambda h: (h, 0, 0, 0)),
        out_shape=jax.ShapeDtypeStruct((nh,) + buckets.shape, F32),
        compiler_params=_params("arbitrary"),
        name="bias_tiles",
    )(rel_bias, jnp.asarray(buckets))


def _softmax_step(s, c, vb, m_ref, l_ref, acc_ref):
    m_prev = m_ref[c]
    m_new = jnp.maximum(m_prev, jnp.max(s, axis=-1, keepdims=True))
    alpha = jnp.exp(m_prev - m_new)
    p = jnp.exp(s - m_new)
    l_ref[c] = alpha * l_ref[c] + jnp.sum(p, axis=-1, keepdims=True)
    acc_ref[c] = alpha * acc_ref[c] + _mm(p, vb)
    m_ref[c] = m_new


LAM_INIT_0 = 0.8 - 0.6 * math.exp(-0.3 * 0)


def _diff_lambda(lv_ref):
    s1 = jnp.sum(lv_ref[0:1, :] * lv_ref[1:2, :], axis=-1, keepdims=True)
    s2 = jnp.sum(lv_ref[2:3, :] * lv_ref[3:4, :], axis=-1, keepdims=True)
    return jnp.exp(s1) - jnp.exp(s2) + LAM_INIT_0


def _diff_attn_kernel(tab_ref, lv_ref, q_ref, k_ref, v_ref, bias_ref, g_ref, o_ref, m_ref, l_ref, acc_ref):
    h = pl.program_id(1)
    qi = pl.program_id(2)
    t = ATTN_TILE
    far = tab_ref[N_BUCKETS - 1, h]
    q = q_ref[...] * (DA ** -0.5)
    lane = lax.broadcasted_iota(jnp.int32, q.shape, 1)
    qc = (jnp.where(lane < DA, q, 0.0).astype(MXU_DTYPE), jnp.where(lane >= DA, q, 0.0).astype(MXU_DTYPE))
    m_ref[...] = jnp.full(m_ref.shape, NEG, F32)
    l_ref[...] = jnp.zeros(l_ref.shape, F32)
    acc_ref[...] = jnp.zeros(acc_ref.shape, F32)

    def step(kj, bias, causal):
        start = pl.multiple_of(kj * t, t)
        kb = k_ref[pl.ds(start, t), :].astype(MXU_DTYPE)
        vb = v_ref[pl.ds(start, t), :].astype(MXU_DTYPE)
        for c in range(2):
            s = _mm_nt(qc[c], kb) + bias
            if causal:
                row = lax.broadcasted_iota(jnp.int32, s.shape, 0)
                col = lax.broadcasted_iota(jnp.int32, s.shape, 1)
                s = jnp.where(col <= row, s, NEG)
            _softmax_step(s, c, vb, m_ref, l_ref, acc_ref)

    def far_body(kj, carry):
        step(kj, far, False)
        return carry

    lax.fori_loop(0, jnp.maximum(qi - 1, 0), far_body, 0)

    @pl.when(qi >= 1)
    def _():
        step(qi - 1, bias_ref[1], False)

    step(qi, bias_ref[0], True)
    o = acc_ref[0] / l_ref[0] - _diff_lambda(lv_ref) * (acc_ref[1] / l_ref[1])
    o_ref[...] = (_rms(o, g_ref[...]) * (1.0 - LAM_INIT_0)).astype(o_ref.dtype)


def _diff_attn_prompt(rel_bias, lam_vecs, q, k, v, bias, subln_g):
    b, tt, _ = q.shape
    t = ATTN_TILE
    assert tt % t == 0
    return pl.pallas_call(
        _diff_attn_kernel,
        grid=(b, HA, tt // t),
        in_specs=[pl.BlockSpec(memory_space=pltpu.SMEM),
                  pl.BlockSpec((4, DA), lambda bi, h, qi: (0, 0)),
                  pl.BlockSpec((None, t, DVA), lambda bi, h, qi: (bi, qi, h)),
                  pl.BlockSpec((None, tt, DVA), lambda bi, h, qi: (bi, 0, h)),
                  pl.BlockSpec((None, tt, DVA), lambda bi, h, qi: (bi, 0, h)),
                  pl.BlockSpec((None, 2, t, t), lambda bi, h, qi: (h, 0, 0, 0)),
                  pl.BlockSpec((1, DVA), lambda bi, h, qi: (0, 0))],
        out_specs=pl.BlockSpec((None, t, DVA), lambda bi, h, qi: (bi, qi, h)),
        out_shape=jax.ShapeDtypeStruct((b, tt, A_V), BF16),
        scratch_shapes=[pltpu.VMEM((2, t, 1), F32), pltpu.VMEM((2, t, 1), F32), pltpu.VMEM((2, t, DVA), F32)],
        compiler_params=_params("parallel", "parallel", "arbitrary"),
        name="diff_attn_prompt",
    )(rel_bias, lam_vecs, q, k, v, bias, subln_g.reshape(1, DVA))


def _moba_select(gate_t, own):
    nb = gate_t.shape[0]
    blk = lax.broadcasted_iota(jnp.int32, gate_t.shape, 0)
    rank = jnp.zeros(gate_t.shape, F32)
    for m in range(nb):
        gm = gate_t[m:m + 1, :]
        beats = jnp.where(gm > gate_t, 1.0, jnp.where(gm == gate_t, jnp.where(m < blk, 1.0, 0.0), 0.0))
        rank = rank + jnp.where(m < own, beats, 0.0)
    return jnp.where(blk < own, jnp.where(rank < MOBA_TOPK, 1.0, 0.0), 0.0)


def _moba_kernel(tab_ref, q_ref, k_ref, v_ref, bias_ref, o_ref, means_ref, m_ref, l_ref, acc_ref, *, nb):
    h = pl.program_id(1)
    qi = pl.program_id(2)
    t = ATTN_TILE
    far = tab_ref[N_BUCKETS - 1, HA + h]

    @pl.when(qi == 0)
    def _():
        for j in range(nb):
            means_ref[j:j + 1, :] = jnp.sum(k_ref[j * t:(j + 1) * t, :], axis=0, keepdims=True) * (1.0 / MOBA_BLOCK)

    qf = q_ref[...]
    gate_t = lax.dot_general(means_ref[...], qf, (((1,), (1,)), ((), ())), precision=lax.Precision.HIGHEST,
                             preferred_element_type=F32)
    sel = _moba_select(gate_t, qi).T
    q = qf.astype(MXU_DTYPE)
    m_ref[...] = jnp.full(m_ref.shape, NEG, F32)
    l_ref[...] = jnp.zeros(l_ref.shape, F32)
    acc_ref[...] = jnp.zeros(acc_ref.shape, F32)

    def step(kj, bias, row_ok):
        start = pl.multiple_of(kj * t, t)
        kb = k_ref[pl.ds(start, t), :].astype(MXU_DTYPE)
        vb = v_ref[pl.ds(start, t), :].astype(MXU_DTYPE)
        s = _mm_nt(q, kb) * (DB ** -0.5) + bias
        if row_ok is None:
            row = lax.broadcasted_iota(jnp.int32, s.shape, 0)
            col = lax.broadcasted_iota(jnp.int32, s.shape, 1)
            s = jnp.where(col <= row, s, NEG)
        else:
            s = jnp.where(row_ok > 0.5, s, NEG)
        _softmax_step(s, 0, vb, m_ref, l_ref, acc_ref)

    step(qi, bias_ref[0], None)
    for kj in range(nb - 1):
        @pl.when(kj < qi - 1)
        def _():
            step(kj, far, sel[:, kj:kj + 1])

        @pl.when(kj == qi - 1)
        def _():
            step(kj, bias_ref[1], sel[:, kj:kj + 1])

    o_ref[...] = (acc_ref[0] / l_ref[0]).astype(o_ref.dtype)


def _moba_prompt(rel_bias, q, k, v, bias):
    b, tt, _ = q.shape
    t = ATTN_TILE
    assert tt % t == 0 and t == MOBA_BLOCK
    nb = tt // t
    return pl.pallas_call(
        functools.partial(_moba_kernel, nb=nb),
        grid=(b, HB, nb),
        in_specs=[pl.BlockSpec(memory_space=pltpu.SMEM),
                  pl.BlockSpec((None, t, DB), lambda bi, h, qi: (bi, qi, h)),
                  pl.BlockSpec((None, tt, DB), lambda bi, h, qi: (bi, 0, h)),
                  pl.BlockSpec((None, tt, DB), lambda bi, h, qi: (bi, 0, h)),
                  pl.BlockSpec((None, 2, t, t), lambda bi, h, qi: (HA + h, 0, 0, 0))],
        out_specs=pl.BlockSpec((None, t, DB), lambda bi, h, qi: (bi, qi, h)),
        out_shape=jax.ShapeDtypeStruct((b, tt, B_W), BF16),
        scratch_shapes=[pltpu.VMEM((nb, DB), F32), pltpu.VMEM((1, t, 1), F32), pltpu.VMEM((1, t, 1), F32),
                        pltpu.VMEM((1, t, DB), F32)],
        compiler_params=_params("parallel", "parallel", "arbitrary"),
        name="moba_prompt",
    )(rel_bias, q, k, v, bias)


def _prompt_bias_buckets():
    t = ATTN_TILE
    r = np.arange(t)[:, None]
    c = np.arange(t)[None, :]
    return np.stack([_t5_bucket_np(r - c), _t5_bucket_np(t + r - c)]).astype(np.int32)


def _matmul_residual_kernel(res_ref, a_ref, b_ref, w_ref, o_ref):
    ka = a_ref.shape[1]
    o_ref[...] = res_ref[...] + _mm(a_ref[...], w_ref[:ka, :]) + _mm(b_ref[...], w_ref[ka:, :])


def _matmul_residual(res, a, b, w, tm):
    n, d = res.shape
    tm = min(tm, n)
    assert n % tm == 0
    return pl.pallas_call(
        _matmul_residual_kernel,
        grid=(n // tm,),
        in_specs=[pl.BlockSpec((tm, d), lambda i: (i, 0)),
                  pl.BlockSpec((tm, a.shape[1]), lambda i: (i, 0)),
                  pl.BlockSpec((tm, b.shape[1]), lambda i: (i, 0)),
                  pl.BlockSpec(w.shape, lambda i: (0, 0))],
        out_specs=pl.BlockSpec((tm, d), lambda i: (i, 0)),
        out_shape=jax.ShapeDtypeStruct((n, d), F32),
        compiler_params=_params("parallel"),
        name="matmul_residual",
    )(res, a, b, w.astype(MXU_DTYPE))


def _silu(x):
    return x * jax.nn.sigmoid(x)


def _ffn_kernel(x_ref, g_ref, wg_ref, wu_ref, wd_ref, o_ref, xn_ref, acc_ref):
    f = pl.program_id(1)

    @pl.when(f == 0)
    def _():
        xn_ref[...] = _rms(x_ref[...], g_ref[...]).astype(xn_ref.dtype)
        acc_ref[...] = jnp.zeros(acc_ref.shape, F32)

    xn = xn_ref[...]
    a = _silu(_mm(xn, wg_ref[...])) * _mm(xn, wu_ref[...])
    acc_ref[...] += _mm(a, wd_ref[...])

    @pl.when(f == pl.num_programs(1) - 1)
    def _():
        o_ref[...] = x_ref[...] + acc_ref[...]


def _ffn(x, g, wg, wu, wd, tm, tf):
    n, d = x.shape
    ff = wg.shape[1]
    tm = min(tm, n)
    assert n % tm == 0 and ff % tf == 0
    return pl.pallas_call(
        _ffn_kernel,
        grid=(n // tm, ff // tf),
        in_specs=[pl.BlockSpec((tm, d), lambda i, f: (i, 0)),
                  pl.BlockSpec((1, d), lambda i, f: (0, 0)),
                  pl.BlockSpec((d, tf), lambda i, f: (0, f)),
                  pl.BlockSpec((d, tf), lambda i, f: (0, f)),
                  pl.BlockSpec((tf, d), lambda i, f: (f, 0))],
        out_specs=pl.BlockSpec((tm, d), lambda i, f: (i, 0)),
        out_shape=jax.ShapeDtypeStruct((n, d), F32),
        scratch_shapes=[pltpu.VMEM((tm, d), MXU_DTYPE), pltpu.VMEM((tm, d), F32)],
        compiler_params=_params("parallel", "arbitrary"),
        name="ffn",
    )(x, g.reshape(1, d), wg.astype(MXU_DTYPE), wu.astype(MXU_DTYPE), wd.astype(MXU_DTYPE))


def _router_gates(logits):
    ne = logits.shape[1]
    lane = lax.broadcasted_iota(jnp.int32, logits.shape, 1)
    m1 = jnp.max(logits, axis=-1, keepdims=True)
    i1 = jnp.min(jnp.where(logits == m1, lane, ne), axis=-1, keepdims=True)
    rest = jnp.where(lane == i1, -jnp.inf, logits)
    m2 = jnp.max(rest, axis=-1, keepdims=True)
    i2 = jnp.min(jnp.where(rest == m2, lane, ne), axis=-1, keepdims=True)
    e2 = jnp.exp(m2 - m1)
    den = 1.0 + e2
    return jnp.where(lane == i1, 1.0 / den, 0.0) + jnp.where(lane == i2, e2 / den, 0.0)


def _moe_dense_kernel(x_ref, g_ref, wr_ref, wg_ref, wu_ref, wd_ref, gf_ref, o_ref, xn_ref, gate_ref, acc_ref):
    e = pl.program_id(1)
    f = pl.program_id(2)

    @pl.when((e == 0) & (f == 0))
    def _():
        xn = _rms(x_ref[...], g_ref[...])
        xn_ref[...] = xn.astype(xn_ref.dtype)
        logits = jnp.dot(xn, wr_ref[...], precision=lax.Precision.HIGHEST, preferred_element_type=F32)
        gate_ref[...] = _router_gates(logits)
        acc_ref[...] = jnp.zeros(acc_ref.shape, F32)

    xn = xn_ref[...]
    a = _silu(_mm(xn, wg_ref[...])) * _mm(xn, wu_ref[...])
    lane = lax.broadcasted_iota(jnp.int32, gate_ref.shape, 1)
    gate_e = jnp.sum(jnp.where(lane == e, gate_ref[...], 0.0), axis=-1, keepdims=True)
    acc_ref[...] += _mm(a, wd_ref[...]) * gate_e

    @pl.when((e == pl.num_programs(1) - 1) & (f == pl.num_programs(2) - 1))
    def _():
        o_ref[...] = _rms(x_ref[...] + acc_ref[...], gf_ref[...])


def _moe_final(x, g, w_router, wg, wu, wd, g_final, tm, tf):
    n, d = x.shape
    ne, _, ff = wg.shape
    tm = min(tm, n)
    assert n % tm == 0 and ff % tf == 0
    return pl.pallas_call(
        _moe_dense_kernel,
        grid=(n // tm, ne, ff // tf),
        in_specs=[pl.BlockSpec((tm, d), lambda i, e, f: (i, 0)),
                  pl.BlockSpec((1, d), lambda i, e, f: (0, 0)),
                  pl.BlockSpec((d, ne), lambda i, e, f: (0, 0)),
                  pl.BlockSpec((None, d, tf), lambda i, e, f: (e, 0, f)),
                  pl.BlockSpec((None, d, tf), lambda i, e, f: (e, 0, f)),
                  pl.BlockSpec((None, tf, d), lambda i, e, f: (e, f, 0)),
                  pl.BlockSpec((1, d), lambda i, e, f: (0, 0))],
        out_specs=pl.BlockSpec((tm, d), lambda i, e, f: (i, 0)),
        out_shape=jax.ShapeDtypeStruct((n, d), F32),
        scratch_shapes=[pltpu.VMEM((tm, d), MXU_DTYPE), pltpu.VMEM((tm, ne), F32), pltpu.VMEM((tm, d), F32)],
        compiler_params=_params("parallel", "arbitrary", "arbitrary"),
        name="moe_final",
    )(x, g.reshape(1, d), w_router, wg.astype(MXU_DTYPE), wu.astype(MXU_DTYPE), wd.astype(MXU_DTYPE),
      g_final.reshape(1, d))


ODD_RWKV_COLS = 3 * D_W + 256
ODD_SPLITS = (C_K, C_K, C_V, C_V, D_W, D_W, D_W, 256)


def _odd_weight_layout(w_in, mu):
    o = HGRN_IN
    r = slice(o, o + D_W)
    wd = slice(o + D_W, o + D_W + D_DECAY_LORA)
    k = slice(wd.stop, wd.stop + D_W)
    v = slice(k.stop, k.stop + D_W)
    rest = slice(v.stop, v.stop + D_AAA_LORA + D_GATE_LORA)
    pad = 256 - (D_DECAY_LORA + D_AAA_LORA + D_GATE_LORA)
    w = jnp.concatenate([w_in[:, :o], w_in[:, r], w_in[:, k], w_in[:, v], w_in[:, wd], w_in[:, rest],
                         jnp.zeros((w_in.shape[0], pad), w_in.dtype)], axis=1)
    ro = lambda s: slice(s.start - o, s.stop - o)
    m = jnp.concatenate([mu[ro(r)], mu[ro(k)], mu[ro(v)], mu[ro(wd)], mu[ro(rest)], jnp.zeros((pad,), mu.dtype)])
    return w, m


def _odd_in_proj_kernel(x_ref, g_ref, w_ref, mu_ref, *refs, shift, tiles_per_seq):
    if shift:
        (cq, cf, ci, cg, r, k, v, lo, xs_ref, carry_ref) = refs
        xn = _rms(x_ref[...], g_ref[...])
        xs_ref[...] = xn[xn.shape[0] - 1:, :]
    else:
        (xp_ref, cq, cf, ci, cg, r, k, v, lo, xs_ref) = refs
        xn = _rms(x_ref[...], g_ref[...])
        xs_ref[...] = xn
    y = xn.astype(MXU_DTYPE)
    off = 0
    for o_ref in (cq, cf, ci, cg):
        o_ref[...] = jnp.dot(y, w_ref[:, off:off + C_K], preferred_element_type=F32)
        off += C_K
    z = jnp.dot(y, w_ref[:, off:], preferred_element_type=F32)
    if shift:
        i = pl.program_id(0)

        @pl.when(i % tiles_per_seq == 0)
        def _():
            carry_ref[...] = jnp.zeros(carry_ref.shape, F32)

        row = lax.broadcasted_iota(jnp.int32, z.shape, 0)
        zp = jnp.where(row == 0, carry_ref[...], pltpu.roll(z, 1, 0))
        carry_ref[...] = z[z.shape[0] - 1:, :]
    else:
        zp = jnp.dot(xp_ref[...].astype(MXU_DTYPE), w_ref[:, off:], preferred_element_type=F32)
    z = z + (zp - z) * mu_ref[...]
    r[...] = z[:, :D_W]
    k[...] = z[:, D_W:2 * D_W]
    v[...] = z[:, 2 * D_W:3 * D_W]
    lo[...] = z[:, 3 * D_W:]


def _odd_in_proj(x, g, w, mu, tm, seq_len=None, x_prev=None):
    n, d = x.shape
    tm = min(tm, n)
    shift = x_prev is None
    assert n % tm == 0
    if shift:
        assert seq_len % tm == 0
    tiles_per_seq = seq_len // tm if shift else 1
    outs = [jax.ShapeDtypeStruct((n, s), F32) for s in ODD_SPLITS]
    out_specs = [pl.BlockSpec((tm, s), lambda i: (i, 0)) for s in ODD_SPLITS]
    in_specs = [pl.BlockSpec((tm, d), lambda i: (i, 0)),
                pl.BlockSpec((1, d), lambda i: (0, 0)),
                pl.BlockSpec(w.shape, lambda i: (0, 0)),
                pl.BlockSpec((1, ODD_RWKV_COLS), lambda i: (0, 0))]
    args = [x, g.reshape(1, d), w.astype(MXU_DTYPE), mu.reshape(1, ODD_RWKV_COLS)]
    scratch = []
    if shift:
        nseq = n // seq_len
        outs.append(jax.ShapeDtypeStruct((nseq, 1, d), F32))
        out_specs.append(pl.BlockSpec((None, 1, d), lambda i: (i // tiles_per_seq, 0, 0)))
        scratch.append(pltpu.VMEM((1, ODD_RWKV_COLS), F32))
    else:
        in_specs.append(pl.BlockSpec((tm, d), lambda i: (i, 0)))
        args.append(x_prev)
        outs.append(jax.ShapeDtypeStruct((n, d), F32))
        out_specs.append(pl.BlockSpec((tm, d), lambda i: (i, 0)))
    return pl.pallas_call(
        functools.partial(_odd_in_proj_kernel, shift=shift, tiles_per_seq=tiles_per_seq),
        grid=(n // tm,),
        in_specs=in_specs,
        out_specs=out_specs,
        out_shape=outs,
        scratch_shapes=scratch,
        compiler_params=_params("arbitrary"),
        name="odd_in_proj",
    )(*args)


def _split3(x):
    if MXU_DTYPE == F32:
        return (x,)
    x1 = x.astype(MXU_DTYPE)
    r1 = x - x1.astype(F32)
    x2 = r1.astype(MXU_DTYPE)
    x3 = (r1 - x2.astype(F32)).astype(MXU_DTYPE)
    return (x1, x2, x3)


def _mm_exact_lhs(ones, x):
    o = ones.astype(MXU_DTYPE)
    return sum(jnp.dot(o, p, preferred_element_type=F32) for p in _split3(x))


def _mm_exact_rhs(x, ones):
    o = ones.astype(MXU_DTYPE)
    return sum(jnp.dot(p, o, preferred_element_type=F32) for p in _split3(x))


def _mm_x3(a, b):
    if MXU_DTYPE == F32:
        return jnp.dot(a, b, preferred_element_type=F32)
    a1 = a.astype(MXU_DTYPE)
    a2 = (a - a1.astype(F32)).astype(MXU_DTYPE)
    b1 = b.astype(MXU_DTYPE)
    b2 = (b - b1.astype(F32)).astype(MXU_DTYPE)
    d = functools.partial(jnp.dot, preferred_element_type=F32)
    return d(a1, b1) + (d(a1, b2) + d(a2, b1))


def _iota2(shape, dim):
    return lax.broadcasted_iota(jnp.int32, shape, dim)


def _group_ones(n, width):
    return jnp.where(_iota2((n, n), 0) // width == _iota2((n, n), 1) // width, 1.0, 0.0)


HGRN_CHUNK = 32
REC_TILE = 256


def _hgrn_forget(cf, lb_ref):
    l0 = lb_ref[0:1, :]
    l1 = lb_ref[1:2, :]
    mx = jnp.maximum(l0, l1)
    e0 = jnp.exp(l0 - mx)
    e1 = jnp.exp(l1 - mx)
    w0 = e0 / (e0 + e1)
    w1 = e1 / (e0 + e1)
    lb = (w0 + w1) - w0
    return lb + (1.0 - lb) * jax.nn.sigmoid(cf)


def _hgrn_kernel(cq_ref, cf_ref, ci_ref, cg_ref, lb_ref, g_ref, o_ref, s_out_ref, s_ref):
    si = pl.program_id(1)
    t = REC_TILE
    c = HGRN_CHUNK
    nc = t // c

    @pl.when(si == 0)
    def _():
        s_ref[...] = jnp.zeros(s_ref.shape, F32)

    f = _hgrn_forget(cf_ref[...], lb_ref)
    q = _silu(cq_ref[...])
    k = 1.0 - f
    logf = jnp.log(f)
    inp = ci_ref[...]
    row = _iota2((t, t), 0)
    col = _iota2((t, t), 1)
    same = row // c == col // c
    cum_m = jnp.where(same & (col <= row), 1.0, 0.0)
    ref_m = jnp.where(same & (col % c <= c // 2), 1.0, 0.0)
    last_m = jnp.where(same, 1.0, 0.0)
    b = _mm_exact_lhs(cum_m, logf)
    b_ref = _mm_exact_lhs(ref_m, logf)
    b_last = _mm_exact_lhs(last_m, logf)
    sel8 = jnp.where(_iota2((nc, t), 0) == _iota2((nc, t), 1) // c, 1.0, 0.0)
    dec_t = jnp.exp(_mm_exact_lhs(sel8, logf)).T
    qs = (q * jnp.exp(b - b_ref)).astype(MXU_DTYPE)
    ks = (k * jnp.exp(b_ref - b)).astype(MXU_DTYPE)
    q_inter = (q * jnp.exp(b)).astype(MXU_DTYPE)
    k_state = (k * jnp.exp(b_last - b)).astype(MXU_DTYPE)
    inp_m = inp.astype(MXU_DTYPE)
    causal = same & (col <= row)
    outs = []
    for h in range(HC):
        cs = slice(h * DKC, (h + 1) * DKC)
        vs = slice(h * DVC, (h + 1) * DVC)
        scores = jnp.where(causal, _mm_nt(qs[:, cs], ks[:, cs]), 0.0)
        o_h = _mm(scores, inp_m[:, vs])
        state = s_ref[h]
        inter = []
        for j in range(nc):
            rs = slice(j * c, (j + 1) * c)
            inter.append(_mm(q_inter[rs, cs], state))
            upd = lax.dot_general(k_state[rs, cs], inp_m[rs, vs], (((0,), (0,)), ((), ())),
                                  preferred_element_type=F32)
            state = state * dec_t[cs, j:j + 1] + upd
        s_ref[h] = state
        o_h = o_h + jnp.concatenate(inter, axis=0)
        outs.append(_rms(o_h, g_ref[...]))
    o = jnp.concatenate(outs, axis=1) * _silu(cg_ref[...])
    o_ref[...] = o.astype(o_ref.dtype)

    @pl.when(si == pl.num_programs(1) - 1)
    def _():
        s_out_ref[...] = s_ref[...]


def _hgrn_prompt(cq, cf, ci, cg, hgrn_lb, norm_g):
    b, tt, _ = cq.shape
    t = REC_TILE
    assert tt % t == 0
    blk = pl.BlockSpec((None, t, C_K), lambda bi, si: (bi, si, 0))
    return pl.pallas_call(
        _hgrn_kernel,
        grid=(b, tt // t),
        in_specs=[blk, blk, blk, blk,
                  pl.BlockSpec(hgrn_lb.shape, lambda bi, si: (0, 0)),
                  pl.BlockSpec((1, DVC), lambda bi, si: (0, 0))],
        out_specs=[pl.BlockSpec((None, t, C_V), lambda bi, si: (bi, si, 0)),
                   pl.BlockSpec((None, HC, DKC, DVC), lambda bi, si: (bi, 0, 0, 0))],
        out_shape=[jax.ShapeDtypeStruct((b, tt, C_V), BF16), jax.ShapeDtypeStruct((b, HC, DKC, DVC), F32)],
        scratch_shapes=[pltpu.VMEM((HC, DKC, DVC), F32)],
        compiler_params=_params("parallel", "arbitrary"),
        name="hgrn_prompt",
    )(cq, cf, ci, cg, hgrn_lb, norm_g.reshape(1, DVC))


RWKV_CHUNK = 32
LORA_COLS = 256


def _rwkv_param_rows(w0, a0, k_k, k_a, r_k, ln_w, ln_b):
    return jnp.stack([w0, a0, k_k, k_a, r_k.reshape(-1), ln_w, ln_b, jnp.zeros_like(w0)]).astype(F32)


def _rwkv_lora_weights(w2, a2, g2):
    z = jnp.zeros((3, LORA_COLS, D_W), F32)
    z = z.at[0, :D_DECAY_LORA].set(w2)
    z = z.at[1, D_DECAY_LORA:D_DECAY_LORA + D_AAA_LORA].set(a2)
    o = D_DECAY_LORA + D_AAA_LORA
    return z.at[2, o:o + D_GATE_LORA].set(g2)


def _dot_hi(a, b):
    return jnp.dot(a, b, precision=lax.Precision.HIGHEST, preferred_element_type=F32)


def _rwkv_activations(r, k, v, lo, par_ref, lw_ref):
    w0, a0, k_k, k_a, r_k = (par_ref[i:i + 1, :] for i in range(5))
    x = -(w0 + _dot_hi(jnp.tanh(lo), lw_ref[0]))
    softplus = jnp.maximum(x, 0.0) + jnp.log(1.0 + jnp.exp(-jnp.abs(x)))
    wlog = -jnp.exp(-softplus - 0.5)
    a = jax.nn.sigmoid(a0 + _dot_hi(lo, lw_ref[1]))
    g = _dot_hi(jax.nn.sigmoid(lo), lw_ref[2])
    ones = _group_ones(D_W, DD)
    kk = k * k_k
    norm = jnp.sqrt(_mm_exact_rhs(kk * kk, ones))
    kk = kk / jnp.maximum(norm, 1e-12)
    k2 = k * (1.0 + (a - 1.0) * k_a)
    bonus = _mm_exact_rhs(r * k2 * r_k, ones) * v
    return wlog, k2, kk, kk * a, g, bonus


def _rwkv_finish(y, bonus, g, par_ref):
    ln_w = par_ref[5:6, :]
    ln_b = par_ref[6:7, :]
    ones = _group_ones(D_W, DD)
    mu = _mm_exact_rhs(y, ones) * (1.0 / DD)
    d = y - mu
    var = _mm_exact_rhs(d * d, ones) * (1.0 / DD)
    return (d * lax.rsqrt(var + RWKV_LN_EPS) * ln_w + ln_b + bonus) * g


def _rwkv_prep_kernel(r_ref, k_ref, v_ref, lo_ref, par_ref, lw_ref, wl_ref, k2_ref, kk_ref, kka_ref, g_ref, bo_ref):
    outs = _rwkv_activations(r_ref[...], k_ref[...], v_ref[...], lo_ref[...], par_ref, lw_ref)
    for o_ref, val in zip((wl_ref, k2_ref, kk_ref, kka_ref, g_ref, bo_ref), outs):
        o_ref[...] = val


def _rwkv_prep(r, k, v, lo, par, lw, tm):
    n = r.shape[0]
    tm = min(tm, n)
    assert n % tm == 0
    blk = pl.BlockSpec((tm, D_W), lambda i: (i, 0))
    return pl.pallas_call(
        _rwkv_prep_kernel,
        grid=(n // tm,),
        in_specs=[blk, blk, blk, pl.BlockSpec((tm, LORA_COLS), lambda i: (i, 0)),
                  pl.BlockSpec(par.shape, lambda i: (0, 0)),
                  pl.BlockSpec(lw.shape, lambda i: (0, 0, 0))],
        out_specs=[blk] * 6,
        out_shape=[jax.ShapeDtypeStruct((n, D_W), F32)] * 6,
        compiler_params=_params("parallel"),
        name="rwkv_prep",
    )(r, k, v, lo, par, lw)


def _pack_heads(x):
    return jnp.concatenate([x[:, h * DD:(h + 1) * DD] for h in range(HD)], axis=0)


def _unpack_heads(x):
    c = x.shape[0] // HD
    return jnp.concatenate([x[h * c:(h + 1) * c, :] for h in range(HD)], axis=1)


def _tile_heads(x):
    c = x.shape[0]
    xt = jnp.concatenate([x] * HD, axis=0)
    keep = _iota2(xt.shape, 0) // c == _iota2(xt.shape, 1) // DD
    return jnp.where(keep, xt, 0.0)


def _mm_tn(a, b):
    return lax.dot_general(a.astype(MXU_DTYPE), b.astype(MXU_DTYPE), (((0,), (0,)), ((), ())),
                           preferred_element_type=F32)


def _rwkv_chunk(r, wl, k, v, al, be, hstate):
    c = r.shape[0]
    n = HD * c
    tri = jnp.where(_iota2((c, c), 1) <= _iota2((c, c), 0), 1.0, 0.0)
    cum = _mm_exact_lhs(tri, wl)
    e_neg = jnp.exp(-cum)
    r_hat = r * jnp.exp(cum)
    a_hat = al * jnp.exp(cum - wl)
    k_til = k * e_neg
    b_til = be * e_neg
    pr, pa, pk, pb, pv = (_pack_heads(x) for x in (r_hat, a_hat, k_til, b_til, v))
    row = _iota2((n, n), 0)
    col = _iota2((n, n), 1)
    same = row // c == col // c
    strict = same & (col < row)
    incl = same & (col <= row)
    nmat = jnp.where(strict, _mm_nt(pa, pb), 0.0)
    a_k = jnp.where(strict, _mm_nt(pa, pk), 0.0)
    r_k = jnp.where(incl, _mm_nt(pr, pk), 0.0)
    r_b = jnp.where(incl, _mm_nt(pr, pb), 0.0)
    inv = jnp.where(row == col, 1.0, 0.0) - nmat
    pw = nmat
    steps = int(math.log2(c))
    assert 2 ** steps == c
    for _ in range(steps - 1):
        pw = _mm_x3(pw, pw)
        inv = inv + _mm_x3(inv, pw)
    w_rhs = _mm(_tile_heads(a_hat), hstate) + _mm(a_k, pv)
    u = _mm_x3(inv, w_rhs)
    y = _mm(_tile_heads(r_hat), hstate) + _mm(r_k, pv) - _mm(r_b, u)
    gam = jnp.broadcast_to(jnp.exp(cum[c - 1:c, :]), (8, HD * DD)).T[:, :1]
    h_new = gam * (hstate + _mm_tn(_tile_heads(k_til), pv) - _mm_tn(_tile_heads(b_til), u))
    return _unpack_heads(y), h_new


def _rwkv_kernel(r_ref, wl_ref, k_ref, v_ref, al_ref, be_ref, g_ref, bo_ref, par_ref, o_ref, s_out_ref, h_ref, y_ref):
    si = pl.program_id(1)
    c = RWKV_CHUNK

    @pl.when(si == 0)
    def _():
        h_ref[...] = jnp.zeros(h_ref.shape, F32)

    def body(j, carry):
        rows = pl.ds(pl.multiple_of(j * c, c), c)
        y, h_new = _rwkv_chunk(r_ref[rows, :], wl_ref[rows, :], k_ref[rows, :], v_ref[rows, :], al_ref[rows, :],
                               be_ref[rows, :], h_ref[...])
        y_ref[rows, :] = y
        h_ref[...] = h_new
        return carry

    lax.fori_loop(0, REC_TILE // c, body, 0)
    o_ref[...] = _rwkv_finish(y_ref[...], bo_ref[...], g_ref[...], par_ref).astype(o_ref.dtype)

    @pl.when(si == pl.num_programs(1) - 1)
    def _():
        s_out_ref[...] = h_ref[...]


def _rwkv_prompt(r, wl, k, v, al, be, g, bonus, par):
    b, tt, _ = r.shape
    t = REC_TILE
    assert tt % t == 0
    blk = pl.BlockSpec((None, t, D_W), lambda bi, si: (bi, si, 0))
    return pl.pallas_call(
        _rwkv_kernel,
        grid=(b, tt // t),
        in_specs=[blk] * 8 + [pl.BlockSpec(par.shape, lambda bi, si: (0, 0))],
        out_specs=[blk, pl.BlockSpec((None, HD * DD, DD), lambda bi, si: (bi, 0, 0))],
        out_shape=[jax.ShapeDtypeStruct((b, tt, D_W), BF16), jax.ShapeDtypeStruct((b, HD * DD, DD), F32)],
        scratch_shapes=[pltpu.VMEM((HD * DD, DD), F32), pltpu.VMEM((t, D_W), F32)],
        compiler_params=_params("parallel", "arbitrary"),
        name="rwkv_prompt",
    )(r, wl, k, v, al, be, g, bonus, par)


def _head_rows(bl_ref, first, count, reps):
    rows = []
    for i in range(count):
        rows += [bl_ref[first + i, 0, 0:1, :]] * reps
    return jnp.concatenate(rows, axis=0) if len(rows) > 1 else rows[0]


def _decode_diff_kernel(pt_ref, tab_ref, lv_ref, q_ref, kn_ref, vn_ref, ck_ref, cv_ref, cm_ref, bl_ref, g_ref,
                        o_ref, sums_ref, m_ref, l_ref, acc_ref, qm_ref):
    p = pl.program_id(1)
    last = pl.num_programs(1) - 1
    nr = 2 * HA
    rowc = _iota2((nr, 1), 0)

    def head_col(bucket):
        col = jnp.zeros((nr, 1), F32)
        for h in range(HA):
            col = jnp.where(rowc // 2 == h, tab_ref[bucket, h], col)
        return col

    @pl.when(p == 0)
    def _():
        q = q_ref[...] * (DA ** -0.5)
        keep = _iota2((nr, A_QK), 1) // DA == _iota2((nr, A_QK), 0)
        qm = jnp.where(keep, jnp.broadcast_to(q, (nr, A_QK)), 0.0)
        qm_ref[...] = qm
        m_ref[...] = jnp.sum(qm * kn_ref[...], axis=-1, keepdims=True) + head_col(0)
        l_ref[...] = jnp.ones(l_ref.shape, F32)
        acc_ref[...] = jnp.broadcast_to(vn_ref[...], acc_ref.shape)

    s = _mm_nt(qm_ref[...], ck_ref[...])
    s = s + jnp.where(p == last, _head_rows(bl_ref, 0, HA, 2), head_col(N_BUCKETS - 1))
    m_prev = m_ref[...]
    m_new = jnp.maximum(m_prev, jnp.max(s, axis=-1, keepdims=True))
    alpha = jnp.exp(m_prev - m_new)
    pe = jnp.exp(s - m_new)
    l_ref[...] = alpha * l_ref[...] + jnp.sum(pe, axis=-1, keepdims=True)
    acc_ref[...] = alpha * acc_ref[...] + _mm(pe, cv_ref[...])
    m_ref[...] = m_new
    sums_ref[pl.ds(p, 1), :] = jnp.sum(cm_ref[...], axis=0, keepdims=True)

    @pl.when(p == last)
    def _():
        outs = acc_ref[...] / l_ref[...]
        lam = _diff_lambda(lv_ref)
        pieces = []
        for h in range(HA):
            cs = slice(h * DVA, (h + 1) * DVA)
            o_h = outs[2 * h:2 * h + 1, cs] - lam * outs[2 * h + 1:2 * h + 2, cs]
            pieces.append(_rms(o_h, g_ref[...]) * (1.0 - LAM_INIT_0))
        o_ref[...] = jnp.concatenate(pieces, axis=1)


def _decode_diff(page_table, rel_bias, lam_vecs, q, k_new, v_new, ckd, cvd, ckm, bias_last, subln_g):
    db, n_pages = page_table.shape
    row = pl.BlockSpec((None, 1, A_V), lambda b, p, pt: (b, 0, 0))
    page = pl.BlockSpec((None, PAGE_SIZE, A_V), lambda b, p, pt: (pt[b * n_pages + p], 0, 0))
    grid_spec = pltpu.PrefetchScalarGridSpec(
        num_scalar_prefetch=1,
        grid=(db, n_pages),
        in_specs=[pl.BlockSpec(memory_space=pltpu.SMEM),
                  pl.BlockSpec((4, DA), lambda b, p, pt: (0, 0)),
                  row, row, row, page, page, page,
                  pl.BlockSpec(bias_last.shape, lambda b, p, pt: (0, 0, 0, 0)),
                  pl.BlockSpec((1, DVA), lambda b, p, pt: (0, 0))],
        out_specs=[row, pl.BlockSpec((None, n_pages, B_W), lambda b, p, pt: (b, 0, 0))],
        scratch_shapes=[pltpu.VMEM((2 * HA, 1), F32), pltpu.VMEM((2 * HA, 1), F32), pltpu.VMEM((2 * HA, A_V), F32),
                        pltpu.VMEM((2 * HA, A_QK), F32)])
    return pl.pallas_call(
        _decode_diff_kernel,
        grid_spec=grid_spec,
        out_shape=[jax.ShapeDtypeStruct((db, 1, A_V), F32), jax.ShapeDtypeStruct((db, n_pages, B_W), F32)],
        compiler_params=_params("parallel", "arbitrary"),
        name="decode_diff",
    )(page_table.reshape(-1), rel_bias, lam_vecs, q, k_new, v_new, ckd, cvd, ckm, bias_last, subln_g.reshape(1, DVA))


SEL_LANES = 128


def _decode_gate_kernel(sums_ref, q_ref, sel_ref):
    n_pages = sums_ref.shape[0]
    ppb = MOBA_BLOCK // PAGE_SIZE
    nb = n_pages // ppb
    pair = jnp.where(_iota2((nb, n_pages), 1) // ppb == _iota2((nb, n_pages), 0), 1.0, 0.0)
    means = _mm_exact_lhs(pair, sums_ref[...]) * (1.0 / MOBA_BLOCK)
    prod = means * q_ref[...]
    blk = _iota2((nb, 1), 0)
    lane = _iota2((1, SEL_LANES), 1)
    out = jnp.zeros((1, SEL_LANES), jnp.int32)
    for h in range(HB):
        gate = jnp.sum(prod[:, h * DB:(h + 1) * DB], axis=-1, keepdims=True)
        for slot in range(MOBA_TOPK):
            best = jnp.max(gate, axis=0, keepdims=True)
            idx = jnp.min(jnp.where(gate == best, blk, nb), axis=0, keepdims=True)
            out = jnp.where(lane == h * MOBA_TOPK + slot, idx, out)
            gate = jnp.where(blk == idx, -jnp.inf, gate)
    sel_ref[...] = out


def _decode_gate(sums, q):
    db, n_pages, _ = sums.shape
    assert n_pages % (MOBA_BLOCK // PAGE_SIZE) == 0 and n_pages * PAGE_SIZE // MOBA_BLOCK >= MOBA_TOPK
    return pl.pallas_call(
        _decode_gate_kernel,
        grid=(db,),
        in_specs=[pl.BlockSpec((None, n_pages, B_W), lambda b: (b, 0, 0)),
                  pl.BlockSpec((None, 1, B_W), lambda b: (b, 0, 0))],
        out_specs=pl.BlockSpec((None, 1, SEL_LANES), lambda b: (b, 0, 0)),
        out_shape=jax.ShapeDtypeStruct((db, 1, SEL_LANES), jnp.int32),
        compiler_params=_params("parallel"),
        name="decode_gate",
    )(sums, q)


def _decode_moba_kernel(pt_ref, sel_ref, tab_ref, q_ref, kn_ref, vn_ref, ck_ref, cv_ref, bl_ref, o_ref,
                        m_ref, l_ref, acc_ref, *, n_pages):
    b = pl.program_id(0)
    h = pl.program_id(1)
    s = pl.program_id(2)
    ppb = MOBA_BLOCK // PAGE_SIZE
    scale = DB ** -0.5
    q = q_ref[...]

    @pl.when(s == 0)
    def _():
        m_ref[...] = jnp.sum(q * kn_ref[...], axis=-1, keepdims=True) * scale + tab_ref[0, HA + h]
        l_ref[...] = jnp.ones(l_ref.shape, F32)
        acc_ref[...] = vn_ref[...]

    page = sel_ref[b * SEL_LANES + h * MOBA_TOPK + s // ppb] * ppb + s % ppb
    q8 = jnp.broadcast_to(q, (8, DB))
    sc = _mm_nt(q8, ck_ref[...])[0:1, :] * scale
    sc = sc + jnp.where(page == n_pages - 1, bl_ref[0, 0:1, :], tab_ref[N_BUCKETS - 1, HA + h])
    m_prev = m_ref[...]
    m_new = jnp.maximum(m_prev, jnp.max(sc, axis=-1, keepdims=True))
    alpha = jnp.exp(m_prev - m_new)
    pe = jnp.exp(sc - m_new)
    l_ref[...] = alpha * l_ref[...] + jnp.sum(pe, axis=-1, keepdims=True)
    acc_ref[...] = alpha * acc_ref[...] + _mm(jnp.broadcast_to(pe, (8, PAGE_SIZE)), cv_ref[...])[0:1, :]
    m_ref[...] = m_new

    @pl.when(s == pl.num_programs(2) - 1)
    def _():
        o_ref[...] = acc_ref[...] / l_ref[...]


def _decode_moba(page_table, sel, rel_bias, q, k_new, v_new, ckm, cvm, bias_last):
    db, n_pages = page_table.shape
    ppb = MOBA_BLOCK // PAGE_SIZE
    row = pl.BlockSpec((None, 1, DB), lambda b, h, s, pt, sl: (b, 0, h))

    def page_index(b, h, s, pt, sl):
        blk = sl[b * SEL_LANES + h * MOBA_TOPK + s // ppb]
        return (pt[b * n_pages + blk * ppb + s % ppb], 0, h)

    page = pl.BlockSpec((None, PAGE_SIZE, DB), page_index)
    grid_spec = pltpu.PrefetchScalarGridSpec(
        num_scalar_prefetch=2,
        grid=(db, HB, MOBA_TOPK * ppb),
        in_specs=[pl.BlockSpec(memory_space=pltpu.SMEM), row, row, row, page, page,
                  pl.BlockSpec((None, 1, 8, PAGE_SIZE), lambda b, h, s, pt, sl: (HA + h, 0, 0, 0))],
        out_specs=row,
        scratch_shapes=[pltpu.VMEM((1, 1), F32), pltpu.VMEM((1, 1), F32), pltpu.VMEM((1, DB), F32)])
    return pl.pallas_call(
        functools.partial(_decode_moba_kernel, n_pages=n_pages),
        grid_spec=grid_spec,
        out_shape=jax.ShapeDtypeStruct((db, 1, B_W), F32),
        compiler_params=_params("parallel", "parallel", "arbitrary"),
        name="decode_moba",
    )(page_table.reshape(-1), sel.reshape(-1), rel_bias, q, k_new, v_new, ckm, cvm, bias_last)


def _to_col(row):
    n = row.shape[1]
    eye = _iota2((n, n), 0) == _iota2((n, n), 1)
    return jnp.sum(jnp.where(eye, row, 0.0), axis=1, keepdims=True)


def _to_row(col):
    n = col.shape[0]
    eye = _iota2((n, n), 0) == _iota2((n, n), 1)
    return jnp.sum(jnp.where(eye, col, 0.0), axis=0, keepdims=True)


def _odd_step_kernel(cq_ref, cf_ref, ci_ref, cg_ref, r_ref, wl_ref, k_ref, v_ref, kk_ref, kka_ref, g_ref, bo_ref,
                     sh_ref, sr_ref, lb_ref, ng_ref, par_ref, o_ref, sh_out_ref, sr_out_ref):
    f = _hgrn_forget(cf_ref[...], lb_ref)
    q = _silu(cq_ref[...])
    k = 1.0 - f
    inp = ci_ref[...]
    outs = []
    for h in range(HC):
        cs = slice(h * DKC, (h + 1) * DKC)
        vs = slice(h * DVC, (h + 1) * DVC)
        s_new = sh_ref[h] * _to_col(f[:, cs]) + _to_col(k[:, cs]) * inp[:, vs]
        sh_out_ref[h] = s_new
        outs.append(_rms(jnp.sum(s_new * _to_col(q[:, cs]), axis=0, keepdims=True), ng_ref[...]))
    o_c = jnp.concatenate(outs, axis=1) * _silu(cg_ref[...])
    r = r_ref[...]
    w = jnp.exp(wl_ref[...])
    k2 = k_ref[...]
    v = v_ref[...]
    kk = kk_ref[...]
    kka = kka_ref[...]
    ys = []
    for h in range(HD):
        cs = slice(h * DD, (h + 1) * DD)
        s_old = sr_ref[h]
        sa = jnp.sum(s_old * (-kk[:, cs]), axis=1, keepdims=True)
        s_new = s_old * w[:, cs] + sa * kka[:, cs] + _to_col(v[:, cs]) * k2[:, cs]
        sr_out_ref[h] = s_new
        ys.append(_to_row(jnp.sum(s_new * r[:, cs], axis=1, keepdims=True)))
    y = jnp.broadcast_to(jnp.concatenate(ys, axis=1), (8, D_W))
    o_d = _rwkv_finish(y, bo_ref[...], g_ref[...], par_ref)[0:1, :]
    o_ref[...] = jnp.concatenate([o_c, o_d], axis=1)


def _odd_step(rows, s_hgrn, s_rwkv, hgrn_lb, norm_g, par):
    db = s_hgrn.shape[0]
    row = pl.BlockSpec((None, 1, D_W), lambda b: (b, 0, 0))
    sh = pl.BlockSpec((None, HC, DKC, DVC), lambda b: (b, 0, 0, 0))
    sr = pl.BlockSpec((None, HD, DD, DD), lambda b: (b, 0, 0, 0))
    return pl.pallas_call(
        _odd_step_kernel,
        grid=(db,),
        in_specs=[row] * 12 + [sh, sr,
                               pl.BlockSpec(hgrn_lb.shape, lambda b: (0, 0)),
                               pl.BlockSpec((1, DVC), lambda b: (0, 0)),
                               pl.BlockSpec(par.shape, lambda b: (0, 0))],
        out_specs=[pl.BlockSpec((None, 1, C_V + D_W), lambda b: (b, 0, 0)), sh, sr],
        out_shape=[jax.ShapeDtypeStruct((db, 1, C_V + D_W), F32),
                   jax.ShapeDtypeStruct(s_hgrn.shape, F32), jax.ShapeDtypeStruct(s_rwkv.shape, F32)],
        compiler_params=_params("parallel"),
        name="odd_step",
    )(*rows, s_hgrn, s_rwkv, hgrn_lb, norm_g.reshape(1, DVC), par)


TM = 512
FFN_TF = 1408
MOE_TF = 896


def kernel(x_prompt, x_sample, cache_k_diff, cache_v_diff, cache_k_moba, cache_v_moba, state_hgrn, state_rwkv, state_shift, page_table, ln_mix_0, w_in_0, lambda_q1, lambda_k1, lambda_q2, lambda_k2, subln_g, rel_bias, w_out_0, ln_ffn_0, ffn_w_gate, ffn_w_up, ffn_w_down, ln_mix_1, w_in_1, hgrn_lb, hgrn_norm_g, rwkv_mu, rwkv_w0, rwkv_w2, rwkv_a0, rwkv_a2, rwkv_g2, rwkv_k_k, rwkv_k_a, rwkv_r_k, rwkv_ln_w, rwkv_ln_b, w_out_1, ln_ffn_1, moe_router, moe_w_gate, moe_w_up, moe_w_down, ln_final):
    bp, tt, d = x_prompt.shape
    db = x_sample.shape[0]
    n_pages = page_table.shape[1]
    n_phys = cache_k_diff.shape[0]
    assert x_sample.shape[1] == 1 and d == D_MODEL
    n = bp * tt

    lam_vecs = jnp.stack([lambda_q1, lambda_k1, lambda_q2, lambda_k2]).astype(F32)
    bias = _bias_tiles(rel_bias, _prompt_bias_buckets())
    last_page_rel = PAGE_SIZE - np.arange(PAGE_SIZE)
    bias_last = _bias_tiles(rel_bias, np.broadcast_to(_t5_bucket_np(last_page_rel), (1, 8, PAGE_SIZE)).astype(np.int32))
    w1, mu1 = _odd_weight_layout(w_in_1, rwkv_mu)
    par = _rwkv_param_rows(rwkv_w0, rwkv_a0, rwkv_k_k, rwkv_k_a, rwkv_r_k, rwkv_ln_w, rwkv_ln_b)
    lw = _rwkv_lora_weights(rwkv_w2, rwkv_a2, rwkv_g2)
    even_splits = (A_QK, A_QK, A_V, B_W, B_W, B_W)

    xp = x_prompt.reshape(n, d)
    qa, ka, va, qb, kb, vb = _norm_matmul(xp, ln_mix_0, w_in_0, even_splits, (BF16, F32, F32, F32, F32, F32), TM)
    seq = lambda a: a.reshape(bp, tt, a.shape[-1])
    oa = _diff_attn_prompt(rel_bias, lam_vecs, seq(qa), seq(ka), seq(va), bias, subln_g)
    ob = _moba_prompt(rel_bias, seq(qb), seq(kb), seq(vb), bias)
    h = _matmul_residual(xp, oa.reshape(n, A_V), ob.reshape(n, B_W), w_out_0, TM)
    h = _ffn(h, ln_ffn_0, ffn_w_gate, ffn_w_up, ffn_w_down, TM, FFN_TF)
    cq, cf, ci, cg, r, k, v, lo, shift_p = _odd_in_proj(h, ln_mix_1, w1, mu1, TM, seq_len=tt)
    o_c, hgrn_p = _hgrn_prompt(seq(cq), seq(cf), seq(ci), seq(cg), hgrn_lb, hgrn_norm_g)
    wl, k2, kk, kka, g, bonus = _rwkv_prep(r, k, v, lo, par, lw, TM)
    o_d, h_state = _rwkv_prompt(seq(r), seq(wl), seq(k2), seq(v), seq(kk), seq(kka), seq(g), seq(bonus), par)
    rwkv_p = h_state.reshape(bp, HD, DD, DD).transpose(0, 1, 3, 2)
    h = _matmul_residual(h, o_c.reshape(n, C_V), o_d.reshape(n, D_W), w_out_1, TM)
    y_prompt = _moe_final(h, ln_ffn_1, moe_router, moe_w_gate, moe_w_up, moe_w_down, ln_final, TM, MOE_TF)

    xs = x_sample.reshape(db, d)
    qa_s, ka_s, va_s, qb_s, kb_s, vb_s = _norm_matmul(xs, ln_mix_0, w_in_0, even_splits, (F32,) * 6, TM)
    one = lambda a: a.reshape(db, 1, a.shape[-1])
    pages = lambda c: c.reshape(n_phys, PAGE_SIZE, -1)
    oa_s, page_sums = _decode_diff(page_table, rel_bias, lam_vecs, one(qa_s), one(ka_s), one(va_s),
                                   pages(cache_k_diff), pages(cache_v_diff), pages(cache_k_moba), bias_last, subln_g)
    sel = _decode_gate(page_sums, one(qb_s))
    ob_s = _decode_moba(page_table, sel, rel_bias, one(qb_s), one(kb_s), one(vb_s), pages(cache_k_moba),
                        pages(cache_v_moba), bias_last)
    hs = _matmul_residual(xs, oa_s.reshape(db, A_V), ob_s.reshape(db, B_W), w_out_0, TM)
    hs = _ffn(hs, ln_ffn_0, ffn_w_gate, ffn_w_up, ffn_w_down, TM, FFN_TF)
    cq, cf, ci, cg, r, k, v, lo, shift_s = _odd_in_proj(hs, ln_mix_1, w1, mu1, TM, x_prev=state_shift)
    wl, k2, kk, kka, g, bonus = _rwkv_prep(r, k, v, lo, par, lw, TM)
    rows = [one(a) for a in (cq, cf, ci, cg, r, wl, k2, v, kk, kka, g, bonus)]
    o_cd, hgrn_s, rwkv_s = _odd_step(rows, state_hgrn, state_rwkv, hgrn_lb, hgrn_norm_g, par)
    o_cd = o_cd.reshape(db, C_V + D_W)
    hs = _matmul_residual(hs, o_cd[:, :C_V], o_cd[:, C_V:], w_out_1, TM)
    y_sample = _moe_final(hs, ln_ffn_1, moe_router, moe_w_gate, moe_w_up, moe_w_down, ln_final, TM, MOE_TF)

    return (y_prompt.reshape(bp, tt, d), y_sample.reshape(db, 1, d),
            ka.reshape(bp, tt, HA, 2, DA), va.reshape(bp, tt, HA, DVA),
            kb.reshape(bp, tt, HB, DB), vb.reshape(bp, tt, HB, DB),
            hgrn_p, rwkv_p, shift_p.reshape(bp, d),
            ka_s.reshape(db, 1, HA, 2, DA), va_s.reshape(db, 1, HA, DVA),
            kb_s.reshape(db, 1, HB, DB), vb_s.reshape(db, 1, HB, DB),
            hgrn_s, rwkv_s, shift_s)
```

```python
import functools
import math

import jax
import jax.numpy as jnp
import numpy as np
from jax import lax
from jax.experimental import pallas as pl
from jax.experimental.pallas import tpu as pltpu

F32 = jnp.float32
BF16 = jnp.bfloat16
MXU_DTYPE = jnp.bfloat16

D_MODEL = 1024
PAGE_SIZE = 128
HA, DA = 4, 64
DVA = 2 * DA
HB, DB = 4, 128
MOBA_BLOCK = 256
MOBA_TOPK = 3
N_BUCKETS = 32
MAX_DISTANCE = 128
HC, DKC, DVC = 4, 128, 128
HD, DD = 8, 64
D_DECAY_LORA, D_AAA_LORA, D_GATE_LORA = 32, 32, 96
RWKV_LN_EPS = 64e-5
N_EXPERTS = 8
EPS = 1e-6

A_QK = HA * 2 * DA
A_V = HA * DVA
B_W = HB * DB
C_K = HC * DKC
C_V = HC * DVC
HGRN_IN = 2 * C_K + 2 * C_V
D_W = HD * DD
RWKV_IN = 3 * D_W + D_DECAY_LORA + D_AAA_LORA + D_GATE_LORA

NEG = -1e30
VMEM_LIMIT_BYTES = 56 * 1024 * 1024
ATTN_TILE = 256


def _params(*sem):
    return pltpu.CompilerParams(dimension_semantics=sem, vmem_limit_bytes=VMEM_LIMIT_BYTES)


def _mm(a, b):
    return jnp.dot(a.astype(MXU_DTYPE), b.astype(MXU_DTYPE), preferred_element_type=F32)


def _mm_nt(a, b):
    return lax.dot_general(a.astype(MXU_DTYPE), b.astype(MXU_DTYPE), (((1,), (1,)), ((), ())),
                           preferred_element_type=F32)


def _rms(x, g):
    return x * lax.rsqrt(jnp.mean(x * x, axis=-1, keepdims=True) + EPS) * g


def _norm_matmul_kernel(x_ref, g_ref, w_ref, *out_refs, splits):
    y = _rms(x_ref[...], g_ref[...]).astype(MXU_DTYPE)
    off = 0
    for o_ref, width in zip(out_refs, splits):
        o_ref[...] = jnp.dot(y, w_ref[:, off:off + width], preferred_element_type=F32).astype(o_ref.dtype)
        off += width


def _norm_matmul(x, g, w, splits, dtypes, tm):
    n, d = x.shape
    tm = min(tm, n)
    assert n % tm == 0 and sum(splits) == w.shape[1]
    return pl.pallas_call(
        functools.partial(_norm_matmul_kernel, splits=tuple(splits)),
        grid=(n // tm,),
        in_specs=[pl.BlockSpec((tm, d), lambda i: (i, 0)),
                  pl.BlockSpec((1, d), lambda i: (0, 0)),
                  pl.BlockSpec(w.shape, lambda i: (0, 0))],
        out_specs=[pl.BlockSpec((tm, s), lambda i: (i, 0)) for s in splits],
        out_shape=[jax.ShapeDtypeStruct((n, s), dt) for s, dt in zip(splits, dtypes)],
        compiler_params=_params("parallel"),
        name="norm_matmul",
    )(x, g.reshape(1, d), w.astype(MXU_DTYPE))


def _t5_bucket_np(rel):
    max_exact = N_BUCKETS // 2
    n = np.maximum(rel, 0)
    nf = np.maximum(n, max_exact).astype(np.float64)
    v = np.log(nf / max_exact) / math.log(MAX_DISTANCE / max_exact) * (N_BUCKETS - max_exact)
    frac = np.abs(v - np.round(v))
    assert np.all((frac > 1e-6) | (n <= max_exact) | (v >= N_BUCKETS - max_exact - 1e-6))
    large = max_exact + np.floor(v + 1e-9).astype(np.int64)
    return np.where(n < max_exact, n, np.minimum(large, N_BUCKETS - 1)).astype(np.int32)


def _bias_tiles_kernel(tab_ref, bk_ref, o_ref):
    h = pl.program_id(0)
    bk = bk_ref[...]
    acc = jnp.zeros(bk.shape, F32)
    for i in range(N_BUCKETS):
        acc = jnp.where(bk == i, tab_ref[i, h], acc)
    o_ref[0] = acc


def _bias_tiles(rel_bias, buckets):
    nh = rel_bias.shape[1]
    return pl.pallas_call(
        _bias_tiles_kernel,
        grid=(nh,),
        in_specs=[pl.BlockSpec(memory_space=pltpu.SMEM),
                  pl.BlockSpec(buckets.shape, lambda h: (0, 0, 0))],
        out_specs=pl.BlockSpec((1,) + buckets.shape, lambda h: (h, 0, 0, 0)),
        out_shape=jax.ShapeDtypeStruct((nh,) + buckets.shape, F32),
        compiler_params=_params("arbitrary"),
        name="bias_tiles",
    )(rel_bias, jnp.asarray(buckets))


def _softmax_pv(s, vb):
    p = jnp.exp(s - jnp.max(s, axis=-1, keepdims=True))
    return _mm(p, vb) / jnp.sum(p, axis=-1, keepdims=True)


def _causal(s):
    return jnp.where(_iota2(s.shape, 1) <= _iota2(s.shape, 0), s, NEG)


def _per_query_tile(qi, nq, compute):
    for qs in range(nq):
        pl.when(qi == qs)(functools.partial(compute, qs))


LAM_INIT_0 = 0.8 - 0.6 * math.exp(-0.3 * 0)


def _diff_lambda(lv_ref):
    s1 = jnp.sum(lv_ref[0:1, :] * lv_ref[1:2, :], axis=-1, keepdims=True)
    s2 = jnp.sum(lv_ref[2:3, :] * lv_ref[3:4, :], axis=-1, keepdims=True)
    return jnp.exp(s1) - jnp.exp(s2) + LAM_INIT_0


def _diff_attn_kernel(tab_ref, lv_ref, q_ref, k_ref, v_ref, bias_ref, g_ref, o_ref):
    h = pl.program_id(1)
    qi = pl.program_id(2)
    t = ATTN_TILE
    far = tab_ref[N_BUCKETS - 1, h]

    def compute(qs):
        w = (qs + 1) * t
        q = q_ref[...] * (DA ** -0.5)
        lane = _iota2(q.shape, 1)
        kb = k_ref[0:w, :].astype(MXU_DTYPE)
        vb = v_ref[0:w, :].astype(MXU_DTYPE)
        outs = []
        for c in range(2):
            qc = jnp.where((lane >= DA) == (c == 1), q, 0.0).astype(MXU_DTYPE)
            s = _mm_nt(qc, kb)
            pieces = [_causal(s[:, qs * t:] + bias_ref[0])]
            if qs >= 1:
                pieces.insert(0, s[:, (qs - 1) * t:qs * t] + bias_ref[1])
            if qs >= 2:
                pieces.insert(0, s[:, :(qs - 1) * t] + far)
            outs.append(_softmax_pv(jnp.concatenate(pieces, axis=1), vb))
        o = outs[0] - _diff_lambda(lv_ref) * outs[1]
        o_ref[...] = (_rms(o, g_ref[...]) * (1.0 - LAM_INIT_0)).astype(o_ref.dtype)

    _per_query_tile(qi, k_ref.shape[0] // t, compute)


def _diff_attn_prompt(rel_bias, lam_vecs, q, k, v, bias, subln_g):
    b, tt, _ = q.shape
    t = ATTN_TILE
    assert tt % t == 0
    return pl.pallas_call(
        _diff_attn_kernel,
        grid=(b, HA, tt // t),
        in_specs=[pl.BlockSpec(memory_space=pltpu.SMEM),
                  pl.BlockSpec((4, DA), lambda bi, h, qi: (0, 0)),
                  pl.BlockSpec((None, t, DVA), lambda bi, h, qi: (bi, qi, h)),
                  pl.BlockSpec((None, tt, DVA), lambda bi, h, qi: (bi, 0, h)),
                  pl.BlockSpec((None, tt, DVA), lambda bi, h, qi: (bi, 0, h)),
                  pl.BlockSpec((None, 2, t, t), lambda bi, h, qi: (h, 0, 0, 0)),
                  pl.BlockSpec((1, DVA), lambda bi, h, qi: (0, 0))],
        out_specs=pl.BlockSpec((None, t, DVA), lambda bi, h, qi: (bi, qi, h)),
        out_shape=jax.ShapeDtypeStruct((b, tt, A_V), BF16),
        compiler_params=_params("parallel", "parallel", "arbitrary"),
        name="diff_attn_prompt",
    )(rel_bias, lam_vecs, q, k, v, bias, subln_g.reshape(1, DVA))


def _moba_select(gate_t, own):
    nb = gate_t.shape[0]
    blk = lax.broadcasted_iota(jnp.int32, gate_t.shape, 0)
    rank = jnp.zeros(gate_t.shape, F32)
    for m in range(nb):
        gm = gate_t[m:m + 1, :]
        beats = jnp.where(gm > gate_t, 1.0, jnp.where(gm == gate_t, jnp.where(m < blk, 1.0, 0.0), 0.0))
        rank = rank + jnp.where(m < own, beats, 0.0)
    return jnp.where(blk < own, jnp.where(rank < MOBA_TOPK, 1.0, 0.0), 0.0)


def _moba_kernel(tab_ref, q_ref, k_ref, v_ref, bias_ref, o_ref, means_ref, *, nb):
    h = pl.program_id(1)
    qi = pl.program_id(2)
    t = ATTN_TILE
    far = tab_ref[N_BUCKETS - 1, HA + h]

    @pl.when(qi == 0)
    def _():
        for j in range(nb):
            means_ref[j:j + 1, :] = jnp.sum(k_ref[j * t:(j + 1) * t, :], axis=0, keepdims=True) * (1.0 / MOBA_BLOCK)

    qf = q_ref[...]
    gate_t = lax.dot_general(means_ref[...], qf, (((1,), (1,)), ((), ())), precision=lax.Precision.HIGHEST,
                             preferred_element_type=F32)
    sel = _moba_select(gate_t, qi).T
    q = qf.astype(MXU_DTYPE)

    def compute(qs):
        w = (qs + 1) * t
        s = _mm_nt(q, k_ref[0:w, :]) * (DB ** -0.5)
        pieces = []
        for kj in range(qs):
            bias = bias_ref[1] if kj == qs - 1 else far
            pieces.append(jnp.where(sel[:, kj:kj + 1] > 0.5, s[:, kj * t:(kj + 1) * t] + bias, NEG))
        pieces.append(_causal(s[:, qs * t:] + bias_ref[0]))
        o_ref[...] = _softmax_pv(jnp.concatenate(pieces, axis=1), v_ref[0:w, :]).astype(o_ref.dtype)

    _per_query_tile(qi, nb, compute)


def _moba_prompt(rel_bias, q, k, v, bias):
    b, tt, _ = q.shape
    t = ATTN_TILE
    assert tt % t == 0 and t == MOBA_BLOCK
    nb = tt // t
    return pl.pallas_call(
        functools.partial(_moba_kernel, nb=nb),
        grid=(b, HB, nb),
        in_specs=[pl.BlockSpec(memory_space=pltpu.SMEM),
                  pl.BlockSpec((None, t, DB), lambda bi, h, qi: (bi, qi, h)),
                  pl.BlockSpec((None, tt, DB), lambda bi, h, qi: (bi, 0, h)),
                  pl.BlockSpec((None, tt, DB), lambda bi, h, qi: (bi, 0, h)),
                  pl.BlockSpec((None, 2, t, t), lambda bi, h, qi: (HA + h, 0, 0, 0))],
        out_specs=pl.BlockSpec((None, t, DB), lambda bi, h, qi: (bi, qi, h)),
        out_shape=jax.ShapeDtypeStruct((b, tt, B_W), BF16),
        scratch_shapes=[pltpu.VMEM((nb, DB), F32)],
        compiler_params=_params("parallel", "parallel", "arbitrary"),
        name="moba_prompt",
    )(rel_bias, q, k, v, bias)


def _prompt_bias_buckets():
    t = ATTN_TILE
    r = np.arange(t)[:, None]
    c = np.arange(t)[None, :]
    return np.stack([_t5_bucket_np(r - c), _t5_bucket_np(t + r - c)]).astype(np.int32)


def _matmul_residual_kernel(res_ref, a_ref, b_ref, w_ref, o_ref):
    ka = a_ref.shape[1]
    o_ref[...] = res_ref[...] + _mm(a_ref[...], w_ref[:ka, :]) + _mm(b_ref[...], w_ref[ka:, :])


def _matmul_residual(res, a, b, w, tm):
    n, d = res.shape
    tm = min(tm, n)
    assert n % tm == 0
    return pl.pallas_call(
        _matmul_residual_kernel,
        grid=(n // tm,),
        in_specs=[pl.BlockSpec((tm, d), lambda i: (i, 0)),
                  pl.BlockSpec((tm, a.shape[1]), lambda i: (i, 0)),
                  pl.BlockSpec((tm, b.shape[1]), lambda i: (i, 0)),
                  pl.BlockSpec(w.shape, lambda i: (0, 0))],
        out_specs=pl.BlockSpec((tm, d), lambda i: (i, 0)),
        out_shape=jax.ShapeDtypeStruct((n, d), F32),
        compiler_params=_params("parallel"),
        name="matmul_residual",
    )(res, a, b, w.astype(MXU_DTYPE))


def _silu(x):
    return x * jax.nn.sigmoid(x)


def _ffn_kernel(x_ref, g_ref, wg_ref, wu_ref, wd_ref, o_ref, xn_ref, acc_ref):
    f = pl.program_id(1)

    @pl.when(f == 0)
    def _():
        xn_ref[...] = _rms(x_ref[...], g_ref[...]).astype(xn_ref.dtype)
        acc_ref[...] = jnp.zeros(acc_ref.shape, F32)

    xn = xn_ref[...]
    a = _silu(_mm(xn, wg_ref[...])) * _mm(xn, wu_ref[...])
    acc_ref[...] += _mm(a, wd_ref[...])

    @pl.when(f == pl.num_programs(1) - 1)
    def _():
        o_ref[...] = x_ref[...] + acc_ref[...]


def _ffn(x, g, wg, wu, wd, tm, tf):
    n, d = x.shape
    ff = wg.shape[1]
    tm = min(tm, n)
    assert n % tm == 0 and ff % tf == 0
    return pl.pallas_call(
        _ffn_kernel,
        grid=(n // tm, ff // tf),
        in_specs=[pl.BlockSpec((tm, d), lambda i, f: (i, 0)),
                  pl.BlockSpec((1, d), lambda i, f: (0, 0)),
                  pl.BlockSpec((d, tf), lambda i, f: (0, f)),
                  pl.BlockSpec((d, tf), lambda i, f: (0, f)),
                  pl.BlockSpec((tf, d), lambda i, f: (f, 0))],
        out_specs=pl.BlockSpec((tm, d), lambda i, f: (i, 0)),
        out_shape=jax.ShapeDtypeStruct((n, d), F32),
        scratch_shapes=[pltpu.VMEM((tm, d), MXU_DTYPE), pltpu.VMEM((tm, d), F32)],
        compiler_params=_params("parallel", "arbitrary"),
        name="ffn",
    )(x, g.reshape(1, d), wg.astype(MXU_DTYPE), wu.astype(MXU_DTYPE), wd.astype(MXU_DTYPE))


def _router_gates(logits):
    ne = logits.shape[1]
    lane = lax.broadcasted_iota(jnp.int32, logits.shape, 1)
    m1 = jnp.max(logits, axis=-1, keepdims=True)
    i1 = jnp.min(jnp.where(logits == m1, lane, ne), axis=-1, keepdims=True)
    rest = jnp.where(lane == i1, -jnp.inf, logits)
    m2 = jnp.max(rest, axis=-1, keepdims=True)
    i2 = jnp.min(jnp.where(rest == m2, lane, ne), axis=-1, keepdims=True)
    e2 = jnp.exp(m2 - m1)
    den = 1.0 + e2
    return jnp.where(lane == i1, 1.0 / den, 0.0) + jnp.where(lane == i2, e2 / den, 0.0)


def _moe_dense_kernel(x_ref, g_ref, wr_ref, wg_ref, wu_ref, wd_ref, gf_ref, o_ref, xn_ref, gate_ref, acc_ref):
    e = pl.program_id(1)
    f = pl.program_id(2)

    @pl.when((e == 0) & (f == 0))
    def _():
        xn = _rms(x_ref[...], g_ref[...])
        xn_ref[...] = xn.astype(xn_ref.dtype)
        logits = jnp.dot(xn, wr_ref[...], precision=lax.Precision.HIGHEST, preferred_element_type=F32)
        gate_ref[...] = _router_gates(logits)
        acc_ref[...] = jnp.zeros(acc_ref.shape, F32)

    xn = xn_ref[...]
    a = _silu(_mm(xn, wg_ref[...])) * _mm(xn, wu_ref[...])
    lane = lax.broadcasted_iota(jnp.int32, gate_ref.shape, 1)
    gate_e = jnp.sum(jnp.where(lane == e, gate_ref[...], 0.0), axis=-1, keepdims=True)
    acc_ref[...] += _mm(a, wd_ref[...]) * gate_e

    @pl.when((e == pl.num_programs(1) - 1) & (f == pl.num_programs(2) - 1))
    def _():
        o_ref[...] = _rms(x_ref[...] + acc_ref[...], gf_ref[...])


def _moe_final(x, g, w_router, wg, wu, wd, g_final, tm, tf):
    n, d = x.shape
    ne, _, ff = wg.shape
    tm = min(tm, n)
    assert n % tm == 0 and ff % tf == 0
    return pl.pallas_call(
        _moe_dense_kernel,
        grid=(n // tm, ne, ff // tf),
        in_specs=[pl.BlockSpec((tm, d), lambda i, e, f: (i, 0)),
                  pl.BlockSpec((1, d), lambda i, e, f: (0, 0)),
                  pl.BlockSpec((d, ne), lambda i, e, f: (0, 0)),
                  pl.BlockSpec((None, d, tf), lambda i, e, f: (e, 0, f)),
                  pl.BlockSpec((None, d, tf), lambda i, e, f: (e, 0, f)),
                  pl.BlockSpec((None, tf, d), lambda i, e, f: (e, f, 0)),
                  pl.BlockSpec((1, d), lambda i, e, f: (0, 0))],
        out_specs=pl.BlockSpec((tm, d), lambda i, e, f: (i, 0)),
        out_shape=jax.ShapeDtypeStruct((n, d), F32),
        scratch_shapes=[pltpu.VMEM((tm, d), MXU_DTYPE), pltpu.VMEM((tm, ne), F32), pltpu.VMEM((tm, d), F32)],
        compiler_params=_params("parallel", "arbitrary", "arbitrary"),
        name="moe_final",
    )(x, g.reshape(1, d), w_router, wg.astype(MXU_DTYPE), wu.astype(MXU_DTYPE), wd.astype(MXU_DTYPE),
      g_final.reshape(1, d))


MOE_TM = 512
META_ROWS = 8


def _moe_route_kernel(x_ref, g_ref, wrt_ref, xn_ref, mi_ref, mw_ref, cnt_ref, carry_ref):
    i = pl.program_id(0)
    tm = x_ref.shape[0]
    ne = wrt_ref.shape[0]

    @pl.when(i == 0)
    def _():
        carry_ref[...] = jnp.zeros(carry_ref.shape, F32)

    xn = _rms(x_ref[...], g_ref[...])
    xn_ref[...] = xn
    logits = lax.dot_general(wrt_ref[...], xn, (((1,), (1,)), ((), ())), precision=lax.Precision.HIGHEST,
                             preferred_element_type=F32)
    sub = _iota2(logits.shape, 0)
    m1 = jnp.max(logits, axis=0, keepdims=True)
    i1 = jnp.min(jnp.where(logits == m1, sub, ne), axis=0, keepdims=True)
    rest = jnp.where(sub == i1, -jnp.inf, logits)
    m2 = jnp.max(rest, axis=0, keepdims=True)
    i2 = jnp.min(jnp.where(rest == m2, sub, ne), axis=0, keepdims=True)
    e2 = jnp.exp(m2 - m1)
    den = 1.0 + e2
    onehot = jnp.where(sub == i1, 1.0, jnp.where(sub == i2, 1.0, 0.0))
    before = jnp.where(_iota2((tm, tm), 0) < _iota2((tm, tm), 1), 1.0, 0.0)
    rank = jnp.dot(onehot.astype(BF16), before.astype(BF16), preferred_element_type=F32) + carry_ref[:, 0:1]
    r1 = jnp.sum(jnp.where(sub == i1, rank, 0.0), axis=0, keepdims=True)
    r2 = jnp.sum(jnp.where(sub == i2, rank, 0.0), axis=0, keepdims=True)
    zi = jnp.zeros((META_ROWS - 4, tm), jnp.int32)
    mi_ref[...] = jnp.concatenate([i1, i2, r1.astype(jnp.int32), r2.astype(jnp.int32), zi], axis=0)
    mw_ref[...] = jnp.concatenate([1.0 / den, e2 / den, jnp.zeros((META_ROWS - 2, tm), F32)], axis=0)
    carry_ref[...] = carry_ref[...] + jnp.sum(onehot, axis=1, keepdims=True)
    cnt_ref[...] = carry_ref[...]


def _moe_route(x, g, w_router):
    n, d = x.shape
    ne = w_router.shape[1]
    tm = MOE_TM
    assert n % tm == 0 and ne == META_ROWS
    meta = pl.BlockSpec((None, META_ROWS, tm), lambda i: (i, 0, 0))
    return pl.pallas_call(
        _moe_route_kernel,
        grid=(n // tm,),
        in_specs=[pl.BlockSpec((tm, d), lambda i: (i, 0)),
                  pl.BlockSpec((1, d), lambda i: (0, 0)),
                  pl.BlockSpec((ne, d), lambda i: (0, 0))],
        out_specs=[pl.BlockSpec((tm, d), lambda i: (i, 0)), meta, meta,
                   pl.BlockSpec((ne, 128), lambda i: (0, 0))],
        out_shape=[jax.ShapeDtypeStruct((n, d), F32),
                   jax.ShapeDtypeStruct((n // tm, META_ROWS, tm), jnp.int32),
                   jax.ShapeDtypeStruct((n // tm, META_ROWS, tm), F32),
                   jax.ShapeDtypeStruct((ne, 128), F32)],
        scratch_shapes=[pltpu.VMEM((ne, 128), F32)],
        compiler_params=_params("arbitrary"),
        name="moe_route",
    )(x, g.reshape(1, d), w_router.T)


def _row_copy(src, src_row, dst, dst_row, sem):
    return pltpu.make_async_copy(src.at[pl.ds(src_row, 1), :], dst.at[pl.ds(dst_row, 1), :], sem)


def _moe_dispatch_kernel(off_ref, mi_ref, x_ref, xs_in_ref, xs_ref, sem):
    del xs_in_ref
    tm = x_ref.shape[0]

    def start(r, carry):
        for slot in range(2):
            dst = off_ref[mi_ref[slot, r]] + mi_ref[2 + slot, r]
            _row_copy(x_ref, r, xs_ref, dst, sem).start()
        return carry

    lax.fori_loop(0, tm, start, 0, unroll=8)

    def wait(r, carry):
        _row_copy(x_ref, 0, xs_ref, 0, sem).wait()
        return carry

    lax.fori_loop(0, 2 * tm, wait, 0)


def _moe_dispatch(offsets, meta_i, xn, n_sorted):
    n, d = xn.shape
    tm = MOE_TM
    grid_spec = pltpu.PrefetchScalarGridSpec(
        num_scalar_prefetch=1,
        grid=(n // tm,),
        in_specs=[pl.BlockSpec((None, META_ROWS, tm), lambda i, off: (i, 0, 0), memory_space=pltpu.SMEM),
                  pl.BlockSpec((tm, d), lambda i, off: (i, 0)),
                  pl.BlockSpec(memory_space=pl.ANY)],
        out_specs=pl.BlockSpec(memory_space=pl.ANY),
        scratch_shapes=[pltpu.SemaphoreType.DMA(())])
    return pl.pallas_call(
        _moe_dispatch_kernel,
        grid_spec=grid_spec,
        out_shape=jax.ShapeDtypeStruct((n_sorted, d), F32),
        input_output_aliases={3: 0},
        compiler_params=_params("arbitrary"),
        name="moe_dispatch",
    )(offsets, meta_i, xn, jnp.zeros((n_sorted, d), F32))


def _moe_experts_kernel(te_ref, nu_ref, x_ref, wg_ref, wu_ref, wd_ref, o_ref, xb_ref, acc_ref):
    t = pl.program_id(0)
    f = pl.program_id(1)

    @pl.when(t < nu_ref[0])
    def _():
        @pl.when(f == 0)
        def _():
            xb_ref[...] = x_ref[...].astype(xb_ref.dtype)
            acc_ref[...] = jnp.zeros(acc_ref.shape, F32)

        xb = xb_ref[...]
        a = _silu(_mm(xb, wg_ref[...])) * _mm(xb, wu_ref[...])
        acc_ref[...] += _mm(a, wd_ref[...])

        @pl.when(f == pl.num_programs(1) - 1)
        def _():
            o_ref[...] = acc_ref[...]

    @pl.when((t >= nu_ref[0]) & (f == pl.num_programs(1) - 1))
    def _():
        o_ref[...] = jnp.zeros(o_ref.shape, F32)


def _moe_experts(tile_expert, n_used, xs, wg, wu, wd, tf):
    r, d = xs.shape
    ne, _, ff = wg.shape
    tm = MOE_TM
    nf = ff // tf
    assert r % tm == 0 and ff % tf == 0

    def row_tile(t, f, te, nu):
        return (jnp.minimum(t, nu[0] - 1), 0)

    def fcol(t, f, nu):
        return jnp.where(t < nu[0], f, nf - 1)

    grid_spec = pltpu.PrefetchScalarGridSpec(
        num_scalar_prefetch=2,
        grid=(r // tm, nf),
        in_specs=[pl.BlockSpec((tm, d), row_tile),
                  pl.BlockSpec((None, d, tf), lambda t, f, te, nu: (te[t], 0, fcol(t, f, nu))),
                  pl.BlockSpec((None, d, tf), lambda t, f, te, nu: (te[t], 0, fcol(t, f, nu))),
                  pl.BlockSpec((None, tf, d), lambda t, f, te, nu: (te[t], fcol(t, f, nu), 0))],
        out_specs=pl.BlockSpec((tm, d), lambda t, f, te, nu: (t, 0)),
        scratch_shapes=[pltpu.VMEM((tm, d), MXU_DTYPE), pltpu.VMEM((tm, d), F32)])
    return pl.pallas_call(
        _moe_experts_kernel,
        grid_spec=grid_spec,
        out_shape=jax.ShapeDtypeStruct((r, d), F32),
        compiler_params=_params("arbitrary", "arbitrary"),
        name="moe_experts",
    )(tile_expert, n_used, xs, wg.astype(MXU_DTYPE), wu.astype(MXU_DTYPE), wd.astype(MXU_DTYPE))


def _moe_combine_kernel(off_ref, mi_ref, mw_ref, x_ref, ys_ref, gf_ref, o_ref, buf_ref, sem):
    tm = x_ref.shape[0]

    def start(r, carry):
        for slot in range(2):
            src = off_ref[mi_ref[slot, r]] + mi_ref[2 + slot, r]
            _row_copy(ys_ref, src, buf_ref.at[slot], r, sem).start()
        return carry

    lax.fori_loop(0, tm, start, 0, unroll=8)

    def wait(r, carry):
        _row_copy(ys_ref, 0, buf_ref.at[0], 0, sem).wait()
        return carry

    lax.fori_loop(0, 2 * tm, wait, 0)
    w1 = _to_col(mw_ref[0:1, :])
    w2 = _to_col(mw_ref[1:2, :])
    o_ref[...] = _rms(x_ref[...] + (buf_ref[0] * w1 + buf_ref[1] * w2), gf_ref[...])


def _moe_combine(offsets, meta_i, meta_w, x, ys, g_final):
    n, d = x.shape
    tm = MOE_TM
    meta = lambda space: pl.BlockSpec((None, META_ROWS, tm), lambda i, off: (i, 0, 0), memory_space=space)
    grid_spec = pltpu.PrefetchScalarGridSpec(
        num_scalar_prefetch=1,
        grid=(n // tm,),
        in_specs=[meta(pltpu.SMEM), meta(pltpu.VMEM),
                  pl.BlockSpec((tm, d), lambda i, off: (i, 0)),
                  pl.BlockSpec(memory_space=pl.ANY),
                  pl.BlockSpec((1, d), lambda i, off: (0, 0))],
        out_specs=pl.BlockSpec((tm, d), lambda i, off: (i, 0)),
        scratch_shapes=[pltpu.VMEM((2, tm, d), F32), pltpu.SemaphoreType.DMA(())])
    return pl.pallas_call(
        _moe_combine_kernel,
        grid_spec=grid_spec,
        out_shape=jax.ShapeDtypeStruct((n, d), F32),
        compiler_params=_params("arbitrary"),
        name="moe_combine",
    )(offsets, meta_i, meta_w, x, ys, g_final.reshape(1, d))


def _moe_routed_final(x, g, w_router, wg, wu, wd, g_final, tf):
    n, d = x.shape
    ne = w_router.shape[1]
    tm = MOE_TM
    xn, meta_i, meta_w, counts = _moe_route(x, g, w_router)
    counts = counts[:, 0].astype(jnp.int32)
    padded = (counts + tm - 1) // tm * tm
    ends = jnp.cumsum(padded)
    offsets = ends - padded
    n_tiles = (2 * n) // tm + ne
    n_used = (ends[-1] // tm).reshape(1)
    tile_start = jnp.arange(n_tiles, dtype=jnp.int32) * tm
    tile_expert = jnp.sum(tile_start[:, None] >= ends[None, :], axis=1).astype(jnp.int32)
    tile_expert = jnp.minimum(tile_expert, tile_expert[jnp.maximum(n_used[0] - 1, 0)])
    xs = _moe_dispatch(offsets, meta_i, xn, n_tiles * tm)
    ys = _moe_experts(tile_expert, n_used, xs, wg, wu, wd, tf)
    return _moe_combine(offsets, meta_i, meta_w, x, ys, g_final)


ODD_RWKV_COLS = 3 * D_W + 256
ODD_SPLITS = (C_K, C_K, C_V, C_V, D_W, D_W, D_W, 256)


def _odd_weight_layout(w_in, mu):
    o = HGRN_IN
    r = slice(o, o + D_W)
    wd = slice(o + D_W, o + D_W + D_DECAY_LORA)
    k = slice(wd.stop, wd.stop + D_W)
    v = slice(k.stop, k.stop + D_W)
    rest = slice(v.stop, v.stop + D_AAA_LORA + D_GATE_LORA)
    pad = 256 - (D_DECAY_LORA + D_AAA_LORA + D_GATE_LORA)
    w = jnp.concatenate([w_in[:, :o], w_in[:, r], w_in[:, k], w_in[:, v], w_in[:, wd], w_in[:, rest],
                         jnp.zeros((w_in.shape[0], pad), w_in.dtype)], axis=1)
    ro = lambda s: slice(s.start - o, s.stop - o)
    m = jnp.concatenate([mu[ro(r)], mu[ro(k)], mu[ro(v)], mu[ro(wd)], mu[ro(rest)], jnp.zeros((pad,), mu.dtype)])
    return w, m


def _odd_in_proj_kernel(x_ref, g_ref, w_ref, mu_ref, *refs, shift, tiles_per_seq):
    if shift:
        (cq, cf, ci, cg, r, k, v, lo, xs_ref, carry_ref) = refs
        xn = _rms(x_ref[...], g_ref[...])
        xs_ref[...] = xn[xn.shape[0] - 1:, :]
    else:
        (xp_ref, cq, cf, ci, cg, r, k, v, lo, xs_ref) = refs
        xn = _rms(x_ref[...], g_ref[...])
        xs_ref[...] = xn
    y = xn.astype(MXU_DTYPE)
    off = 0
    for o_ref in (cq, cf, ci, cg):
        o_ref[...] = jnp.dot(y, w_ref[:, off:off + C_K], preferred_element_type=F32)
        off += C_K
    z = jnp.dot(y, w_ref[:, off:], preferred_element_type=F32)
    if shift:
        i = pl.program_id(0)

        @pl.when(i % tiles_per_seq == 0)
        def _():
            carry_ref[...] = jnp.zeros(carry_ref.shape, F32)

        row = lax.broadcasted_iota(jnp.int32, z.shape, 0)
        zp = jnp.where(row == 0, carry_ref[...], pltpu.roll(z, 1, 0))
        carry_ref[...] = z[z.shape[0] - 1:, :]
    else:
        zp = jnp.dot(xp_ref[...].astype(MXU_DTYPE), w_ref[:, off:], preferred_element_type=F32)
    z = z + (zp - z) * mu_ref[...]
    r[...] = z[:, :D_W]
    k[...] = z[:, D_W:2 * D_W]
    v[...] = z[:, 2 * D_W:3 * D_W]
    lo[...] = z[:, 3 * D_W:]


def _odd_in_proj(x, g, w, mu, tm, seq_len=None, x_prev=None):
    n, d = x.shape
    tm = min(tm, n)
    shift = x_prev is None
    assert n % tm == 0
    if shift:
        assert seq_len % tm == 0
    tiles_per_seq = seq_len // tm if shift else 1
    outs = [jax.ShapeDtypeStruct((n, s), F32) for s in ODD_SPLITS]
    out_specs = [pl.BlockSpec((tm, s), lambda i: (i, 0)) for s in ODD_SPLITS]
    in_specs = [pl.BlockSpec((tm, d), lambda i: (i, 0)),
                pl.BlockSpec((1, d), lambda i: (0, 0)),
                pl.BlockSpec(w.shape, lambda i: (0, 0)),
                pl.BlockSpec((1, ODD_RWKV_COLS), lambda i: (0, 0))]
    args = [x, g.reshape(1, d), w.astype(MXU_DTYPE), mu.reshape(1, ODD_RWKV_COLS)]
    scratch = []
    if shift:
        nseq = n // seq_len
        outs.append(jax.ShapeDtypeStruct((nseq, 1, d), F32))
        out_specs.append(pl.BlockSpec((None, 1, d), lambda i: (i // tiles_per_seq, 0, 0)))
        scratch.append(pltpu.VMEM((1, ODD_RWKV_COLS), F32))
    else:
        in_specs.append(pl.BlockSpec((tm, d), lambda i: (i, 0)))
        args.append(x_prev)
        outs.append(jax.ShapeDtypeStruct((n, d), F32))
        out_specs.append(pl.BlockSpec((tm, d), lambda i: (i, 0)))
    return pl.pallas_call(
        functools.partial(_odd_in_proj_kernel, shift=shift, tiles_per_seq=tiles_per_seq),
        grid=(n // tm,),
        in_specs=in_specs,
        out_specs=out_specs,
        out_shape=outs,
        scratch_shapes=scratch,
        compiler_params=_params("arbitrary"),
        name="odd_in_proj",
    )(*args)


def _split3(x):
    if MXU_DTYPE == F32:
        return (x,)
    x1 = x.astype(MXU_DTYPE)
    r1 = x - x1.astype(F32)
    x2 = r1.astype(MXU_DTYPE)
    x3 = (r1 - x2.astype(F32)).astype(MXU_DTYPE)
    return (x1, x2, x3)


def _mm_exact_lhs(ones, x):
    o = ones.astype(MXU_DTYPE)
    return sum(jnp.dot(o, p, preferred_element_type=F32) for p in _split3(x))


def _mm_exact_rhs(x, ones):
    o = ones.astype(MXU_DTYPE)
    return sum(jnp.dot(p, o, preferred_element_type=F32) for p in _split3(x))


def _iota2(shape, dim):
    return lax.broadcasted_iota(jnp.int32, shape, dim)


def _group_ones(n, width):
    return jnp.where(_iota2((n, n), 0) // width == _iota2((n, n), 1) // width, 1.0, 0.0)


HGRN_CHUNK = 32
REC_TILE = 256


def _hgrn_forget(cf, lb_ref):
    l0 = lb_ref[0:1, :]
    l1 = lb_ref[1:2, :]
    mx = jnp.maximum(l0, l1)
    e0 = jnp.exp(l0 - mx)
    e1 = jnp.exp(l1 - mx)
    w0 = e0 / (e0 + e1)
    w1 = e1 / (e0 + e1)
    lb = (w0 + w1) - w0
    return lb + (1.0 - lb) * jax.nn.sigmoid(cf)


def _hgrn_kernel(cq_ref, cf_ref, ci_ref, cg_ref, lb_ref, g_ref, o_ref, s_out_ref, s_ref):
    si = pl.program_id(1)
    t = REC_TILE
    c = HGRN_CHUNK
    nc = t // c

    @pl.when(si == 0)
    def _():
        s_ref[...] = jnp.zeros(s_ref.shape, F32)

    f = _hgrn_forget(cf_ref[...], lb_ref)
    q = _silu(cq_ref[...])
    k = 1.0 - f
    logf = jnp.log(f)
    inp = ci_ref[...]
    row = _iota2((t, t), 0)
    col = _iota2((t, t), 1)
    same = row // c == col // c
    cum_m = jnp.where(same & (col <= row), 1.0, 0.0)
    ref_m = jnp.where(same & (col % c <= c // 2), 1.0, 0.0)
    last_m = jnp.where(same, 1.0, 0.0)
    b = _mm_exact_lhs(cum_m, logf)
    b_ref = _mm_exact_lhs(ref_m, logf)
    b_last = _mm_exact_lhs(last_m, logf)
    sel8 = jnp.where(_iota2((nc, t), 0) == _iota2((nc, t), 1) // c, 1.0, 0.0)
    dec_t = jnp.exp(_mm_exact_lhs(sel8, logf)).T
    qs = (q * jnp.exp(b - b_ref)).astype(MXU_DTYPE)
    ks = (k * jnp.exp(b_ref - b)).astype(MXU_DTYPE)
    q_inter = (q * jnp.exp(b)).astype(MXU_DTYPE)
    k_state = (k * jnp.exp(b_last - b)).astype(MXU_DTYPE)
    inp_m = inp.astype(MXU_DTYPE)
    causal = same & (col <= row)
    outs = []
    for h in range(HC):
        cs = slice(h * DKC, (h + 1) * DKC)
        vs = slice(h * DVC, (h + 1) * DVC)
        scores = jnp.where(causal, _mm_nt(qs[:, cs], ks[:, cs]), 0.0)
        o_h = _mm(scores, inp_m[:, vs])
        state = s_ref[h]
        inter = []
        for j in range(nc):
            rs = slice(j * c, (j + 1) * c)
            inter.append(_mm(q_inter[rs, cs], state))
            upd = lax.dot_general(k_state[rs, cs], inp_m[rs, vs], (((0,), (0,)), ((), ())),
                                  preferred_element_type=F32)
            state = state * dec_t[cs, j:j + 1] + upd
        s_ref[h] = state
        o_h = o_h + jnp.concatenate(inter, axis=0)
        outs.append(_rms(o_h, g_ref[...]))
    o = jnp.concatenate(outs, axis=1) * _silu(cg_ref[...])
    o_ref[...] = o.astype(o_ref.dtype)

    @pl.when(si == pl.num_programs(1) - 1)
    def _():
        s_out_ref[...] = s_ref[...]


def _hgrn_prompt(cq, cf, ci, cg, hgrn_lb, norm_g):
    b, tt, _ = cq.shape
    t = REC_TILE
    assert tt % t == 0
    blk = pl.BlockSpec((None, t, C_K), lambda bi, si: (bi, si, 0))
    return pl.pallas_call(
        _hgrn_kernel,
        grid=(b, tt // t),
        in_specs=[blk, blk, blk, blk,
                  pl.BlockSpec(hgrn_lb.shape, lambda bi, si: (0, 0)),
                  pl.BlockSpec((1, DVC), lambda bi, si: (0, 0))],
        out_specs=[pl.BlockSpec((None, t, C_V), lambda bi, si: (bi, si, 0)),
                   pl.BlockSpec((None, HC, DKC, DVC), lambda bi, si: (bi, 0, 0, 0))],
        out_shape=[jax.ShapeDtypeStruct((b, tt, C_V), BF16), jax.ShapeDtypeStruct((b, HC, DKC, DVC), F32)],
        scratch_shapes=[pltpu.VMEM((HC, DKC, DVC), F32)],
        compiler_params=_params("parallel", "arbitrary"),
        name="hgrn_prompt",
    )(cq, cf, ci, cg, hgrn_lb, norm_g.reshape(1, DVC))


RWKV_CHUNK = 32
LORA_COLS = 256


def _rwkv_param_rows(w0, a0, k_k, k_a, r_k, ln_w, ln_b):
    return jnp.stack([w0, a0, k_k, k_a, r_k.reshape(-1), ln_w, ln_b, jnp.zeros_like(w0)]).astype(F32)


def _rwkv_lora_weights(w2, a2, g2):
    z = jnp.zeros((3, LORA_COLS, D_W), F32)
    z = z.at[0, :D_DECAY_LORA].set(w2)
    z = z.at[1, D_DECAY_LORA:D_DECAY_LORA + D_AAA_LORA].set(a2)
    o = D_DECAY_LORA + D_AAA_LORA
    return z.at[2, o:o + D_GATE_LORA].set(g2)


def _dot_hi(a, b):
    return jnp.dot(a, b, precision=lax.Precision.HIGHEST, preferred_element_type=F32)


def _rwkv_activations(r, k, v, lo, par_ref, lw_ref):
    w0, a0, k_k, k_a, r_k = (par_ref[i:i + 1, :] for i in range(5))
    x = -(w0 + _dot_hi(jnp.tanh(lo), lw_ref[0]))
    softplus = jnp.maximum(x, 0.0) + jnp.log(1.0 + jnp.exp(-jnp.abs(x)))
    wlog = -jnp.exp(-softplus - 0.5)
    a = jax.nn.sigmoid(a0 + _dot_hi(lo, lw_ref[1]))
    g = _dot_hi(jax.nn.sigmoid(lo), lw_ref[2])
    ones = _group_ones(D_W, DD)
    kk = k * k_k
    norm = jnp.sqrt(_mm_exact_rhs(kk * kk, ones))
    kk = kk / jnp.maximum(norm, 1e-12)
    k2 = k * (1.0 + (a - 1.0) * k_a)
    bonus = _mm_exact_rhs(r * k2 * r_k, ones) * v
    return wlog, k2, kk, kk * a, g, bonus


def _rwkv_finish(y, bonus, g, par_ref):
    ln_w = par_ref[5:6, :]
    ln_b = par_ref[6:7, :]
    ones = _group_ones(D_W, DD)
    mu = _mm_exact_rhs(y, ones) * (1.0 / DD)
    d = y - mu
    var = _mm_exact_rhs(d * d, ones) * (1.0 / DD)
    return (d * lax.rsqrt(var + RWKV_LN_EPS) * ln_w + ln_b + bonus) * g


def _rwkv_prep_kernel(r_ref, k_ref, v_ref, lo_ref, par_ref, lw_ref, wl_ref, k2_ref, kk_ref, kka_ref, g_ref, bo_ref):
    outs = _rwkv_activations(r_ref[...], k_ref[...], v_ref[...], lo_ref[...], par_ref, lw_ref)
    for o_ref, val in zip((wl_ref, k2_ref, kk_ref, kka_ref, g_ref, bo_ref), outs):
        o_ref[...] = val


def _rwkv_prep(r, k, v, lo, par, lw, tm):
    n = r.shape[0]
    tm = min(tm, n)
    assert n % tm == 0
    blk = pl.BlockSpec((tm, D_W), lambda i: (i, 0))
    return pl.pallas_call(
        _rwkv_prep_kernel,
        grid=(n // tm,),
        in_specs=[blk, blk, blk, pl.BlockSpec((tm, LORA_COLS), lambda i: (i, 0)),
                  pl.BlockSpec(par.shape, lambda i: (0, 0)),
                  pl.BlockSpec(lw.shape, lambda i: (0, 0, 0))],
        out_specs=[blk] * 6,
        out_shape=[jax.ShapeDtypeStruct((n, D_W), F32)] * 6,
        compiler_params=_params("parallel"),
        name="rwkv_prep",
    )(r, k, v, lo, par, lw)


def _pack_heads(x):
    return jnp.concatenate([x[:, h * DD:(h + 1) * DD] for h in range(HD)], axis=0)


def _unpack_heads(x):
    c = x.shape[0] // HD
    return jnp.concatenate([x[h * c:(h + 1) * c, :] for h in range(HD)], axis=1)


def _tile_heads(x):
    c = x.shape[0]
    xt = jnp.concatenate([x] * HD, axis=0)
    keep = _iota2(xt.shape, 0) // c == _iota2(xt.shape, 1) // DD
    return jnp.where(keep, xt, 0.0)


def _mm_tn(a, b):
    return lax.dot_general(a.astype(MXU_DTYPE), b.astype(MXU_DTYPE), (((0,), (0,)), ((), ())),
                           preferred_element_type=F32)


def _rwkv_chunk(r, wl, k, v, al, be, hstate):
    c = r.shape[0]
    n = HD * c
    tri = jnp.where(_iota2((c, c), 1) <= _iota2((c, c), 0), 1.0, 0.0)
    cum = _mm_exact_lhs(tri, wl)
    e_neg = jnp.exp(-cum)
    r_hat = r * jnp.exp(cum)
    a_hat = al * jnp.exp(cum - wl)
    k_til = k * e_neg
    b_til = be * e_neg
    pr, pa, pk, pb, pv = (_pack_heads(x) for x in (r_hat, a_hat, k_til, b_til, v))
    row = _iota2((n, n), 0)
    col = _iota2((n, n), 1)
    same = row // c == col // c
    strict = same & (col < row)
    incl = same & (col <= row)
    nmat = jnp.where(strict, _mm_nt(pa, pb), 0.0)
    a_k = jnp.where(strict, _mm_nt(pa, pk), 0.0)
    r_k = jnp.where(incl, _mm_nt(pr, pk), 0.0)
    r_b = jnp.where(incl, _mm_nt(pr, pb), 0.0)
    inv = jnp.where(row == col, 1.0, 0.0) - nmat
    pw = nmat
    steps = int(math.log2(c))
    assert 2 ** steps == c
    for _ in range(steps - 1):
        pw = _mm(pw, pw)
        inv = inv + _mm(inv, pw)
    w_rhs = _mm(_tile_heads(a_hat), hstate) + _mm(a_k, pv)
    u = _mm(inv, w_rhs)
    y = _mm(_tile_heads(r_hat), hstate) + _mm(r_k, pv) - _mm(r_b, u)
    gam = jnp.broadcast_to(jnp.exp(cum[c - 1:c, :]), (8, HD * DD)).T[:, :1]
    h_new = gam * (hstate + _mm_tn(_tile_heads(k_til), pv) - _mm_tn(_tile_heads(b_til), u))
    return _unpack_heads(y), h_new


def _rwkv_kernel(r_ref, wl_ref, k_ref, v_ref, al_ref, be_ref, g_ref, bo_ref, par_ref, o_ref, s_out_ref, h_ref, y_ref):
    si = pl.program_id(1)
    c = RWKV_CHUNK

    @pl.when(si == 0)
    def _():
        h_ref[...] = jnp.zeros(h_ref.shape, F32)

    def body(j, carry):
        rows = pl.ds(pl.multiple_of(j * c, c), c)
        y, h_new = _rwkv_chunk(r_ref[rows, :], wl_ref[rows, :], k_ref[rows, :], v_ref[rows, :], al_ref[rows, :],
                               be_ref[rows, :], h_ref[...])
        y_ref[rows, :] = y
        h_ref[...] = h_new
        return carry

    lax.fori_loop(0, REC_TILE // c, body, 0)
    o_ref[...] = _rwkv_finish(y_ref[...], bo_ref[...], g_ref[...], par_ref).astype(o_ref.dtype)

    @pl.when(si == pl.num_programs(1) - 1)
    def _():
        s_out_ref[...] = h_ref[...]


def _rwkv_prompt(r, wl, k, v, al, be, g, bonus, par):
    b, tt, _ = r.shape
    t = REC_TILE
    assert tt % t == 0
    blk = pl.BlockSpec((None, t, D_W), lambda bi, si: (bi, si, 0))
    return pl.pallas_call(
        _rwkv_kernel,
        grid=(b, tt // t),
        in_specs=[blk] * 8 + [pl.BlockSpec(par.shape, lambda bi, si: (0, 0))],
        out_specs=[blk, pl.BlockSpec((None, HD * DD, DD), lambda bi, si: (bi, 0, 0))],
        out_shape=[jax.ShapeDtypeStruct((b, tt, D_W), BF16), jax.ShapeDtypeStruct((b, HD * DD, DD), F32)],
        scratch_shapes=[pltpu.VMEM((HD * DD, DD), F32), pltpu.VMEM((t, D_W), F32)],
        compiler_params=_params("parallel", "arbitrary"),
        name="rwkv_prompt",
    )(r, wl, k, v, al, be, g, bonus, par)


def _head_rows(bl_ref, first, count, reps):
    rows = []
    for i in range(count):
        rows += [bl_ref[first + i, 0, 0:1, :]] * reps
    return jnp.concatenate(rows, axis=0) if len(rows) > 1 else rows[0]


DECODE_PAGES = 8


def _decode_diff_kernel(pt_ref, tab_ref, lv_ref, q_ref, kn_ref, vn_ref, *refs):
    pp = DECODE_PAGES
    ck, cv, cm = refs[:pp], refs[pp:2 * pp], refs[2 * pp:3 * pp]
    bl_ref, g_ref, o_ref, sums_ref, m_ref, l_ref, acc_ref, qm_ref = refs[3 * pp:]
    p = pl.program_id(1)
    last = pl.num_programs(1) - 1
    nr = 2 * HA
    rowc = _iota2((nr, 1), 0)

    def head_col(bucket):
        col = jnp.zeros((nr, 1), F32)
        for h in range(HA):
            col = jnp.where(rowc // 2 == h, tab_ref[bucket, h], col)
        return col

    @pl.when(p == 0)
    def _():
        q = q_ref[...] * (DA ** -0.5)
        keep = _iota2((nr, A_QK), 1) // DA == _iota2((nr, A_QK), 0)
        qm = jnp.where(keep, jnp.broadcast_to(q, (nr, A_QK)), 0.0)
        qm_ref[...] = qm
        m_ref[...] = jnp.sum(qm * kn_ref[...], axis=-1, keepdims=True) + head_col(0)
        l_ref[...] = jnp.ones(l_ref.shape, F32)
        for h in range(HA):
            acc_ref[h] = jnp.broadcast_to(vn_ref[:, h * DVA:(h + 1) * DVA], (nr, DVA))

    qm = qm_ref[...].astype(MXU_DTYPE)
    s = jnp.concatenate([jnp.dot(qm, ck[j][...].reshape(A_QK, PAGE_SIZE).astype(MXU_DTYPE),
                                 preferred_element_type=F32) for j in range(pp)], axis=1)
    far = head_col(N_BUCKETS - 1)
    near = jnp.concatenate([jnp.broadcast_to(far, (nr, (pp - 1) * PAGE_SIZE)), _head_rows(bl_ref, 0, HA, 2)], axis=1)
    s = s + jnp.where(p == last, near, far)
    m_prev = m_ref[...]
    m_new = jnp.maximum(m_prev, jnp.max(s, axis=-1, keepdims=True))
    alpha = jnp.exp(m_prev - m_new)
    pe = jnp.exp(s - m_new)
    l_ref[...] = alpha * l_ref[...] + jnp.sum(pe, axis=-1, keepdims=True)
    for h in range(HA):
        vh = jnp.concatenate([cv[j][:, h, :] for j in range(pp)], axis=0)
        acc_ref[h] = alpha * acc_ref[h] + _mm(pe, vh)
    m_ref[...] = m_new
    for j in range(pp):
        sums_ref[j] = jnp.sum(cm[j][...], axis=0)

    @pl.when(p == last)
    def _():
        lam = _diff_lambda(lv_ref)
        pieces = []
        for h in range(HA):
            outs = acc_ref[h] / l_ref[...]
            o_h = outs[2 * h:2 * h + 1, :] - lam * outs[2 * h + 1:2 * h + 2, :]
            pieces.append(_rms(o_h, g_ref[...]) * (1.0 - LAM_INIT_0))
        o_ref[...] = jnp.concatenate(pieces, axis=1)


def _decode_diff(page_table, rel_bias, lam_vecs, q, k_new, v_new, ckd_t, cvd, ckm, bias_last, subln_g):
    db, n_pages = page_table.shape
    pp = DECODE_PAGES
    assert n_pages % pp == 0
    row = pl.BlockSpec((None, 1, A_V), lambda b, p, pt: (b, 0, 0))

    def page_spec(block, j):
        zeros = (0,) * len(block)
        return pl.BlockSpec((None,) + block, lambda b, p, pt: (pt[b * n_pages + p * pp + j],) + zeros)

    kt_pages = [page_spec((HA, 2, DA, PAGE_SIZE), j) for j in range(pp)]
    v_pages = [page_spec((PAGE_SIZE, HA, DVA), j) for j in range(pp)]
    m_pages = [page_spec((PAGE_SIZE, HB, DB), j) for j in range(pp)]
    grid_spec = pltpu.PrefetchScalarGridSpec(
        num_scalar_prefetch=1,
        grid=(db, n_pages // pp),
        in_specs=[pl.BlockSpec(memory_space=pltpu.SMEM),
                  pl.BlockSpec((4, DA), lambda b, p, pt: (0, 0)),
                  row, row, row, *kt_pages, *v_pages, *m_pages,
                  pl.BlockSpec(bias_last.shape, lambda b, p, pt: (0, 0, 0, 0)),
                  pl.BlockSpec((1, DVA), lambda b, p, pt: (0, 0))],
        out_specs=[row, pl.BlockSpec((None, pp, HB, DB), lambda b, p, pt: (b, p, 0, 0))],
        scratch_shapes=[pltpu.VMEM((2 * HA, 1), F32), pltpu.VMEM((2 * HA, 1), F32),
                        pltpu.VMEM((HA, 2 * HA, DVA), F32), pltpu.VMEM((2 * HA, A_QK), F32)])
    return pl.pallas_call(
        _decode_diff_kernel,
        grid_spec=grid_spec,
        out_shape=[jax.ShapeDtypeStruct((db, 1, A_V), F32), jax.ShapeDtypeStruct((db, n_pages, HB, DB), F32)],
        compiler_params=_params("parallel", "arbitrary"),
        name="decode_diff",
    )(page_table.reshape(-1), rel_bias, lam_vecs, q, k_new, v_new, *([ckd_t] * pp), *([cvd] * pp), *([ckm] * pp),
      bias_last, subln_g.reshape(1, DVA))


SEL_LANES = 128


def _decode_gate_kernel(sums_ref, q_ref, sel_ref):
    n_pages = sums_ref.shape[0]
    ppb = MOBA_BLOCK // PAGE_SIZE
    nb = n_pages // ppb
    pair = jnp.where(_iota2((nb, n_pages), 1) // ppb == _iota2((nb, n_pages), 0), 1.0, 0.0)
    blk = _iota2((nb, 1), 0)
    lane = _iota2((1, SEL_LANES), 1)
    out = jnp.zeros((1, SEL_LANES), jnp.int32)
    for h in range(HB):
        means = _mm_exact_lhs(pair, sums_ref[:, h, :]) * (1.0 / MOBA_BLOCK)
        gate = jnp.sum(means * q_ref[:, h * DB:(h + 1) * DB], axis=-1, keepdims=True)
        for slot in range(MOBA_TOPK):
            best = jnp.max(gate, axis=0, keepdims=True)
            idx = jnp.min(jnp.where(gate == best, blk, nb), axis=0, keepdims=True)
            out = jnp.where(lane == h * MOBA_TOPK + slot, idx, out)
            gate = jnp.where(blk == idx, -jnp.inf, gate)
    sel_ref[...] = out


def _decode_gate(sums, q):
    db, n_pages = sums.shape[:2]
    assert n_pages % (MOBA_BLOCK // PAGE_SIZE) == 0 and n_pages * PAGE_SIZE // MOBA_BLOCK >= MOBA_TOPK
    return pl.pallas_call(
        _decode_gate_kernel,
        grid=(db,),
        in_specs=[pl.BlockSpec((None, n_pages, HB, DB), lambda b: (b, 0, 0, 0)),
                  pl.BlockSpec((None, 1, B_W), lambda b: (b, 0, 0))],
        out_specs=pl.BlockSpec((None, 1, SEL_LANES), lambda b: (b, 0, 0)),
        out_shape=jax.ShapeDtypeStruct((db, 1, SEL_LANES), jnp.int32),
        compiler_params=_params("parallel"),
        name="decode_gate",
    )(sums, q)


def _decode_moba_kernel(pt_ref, sel_ref, tab_ref, q_ref, kn_ref, vn_ref, *refs, n_pages):
    ppb = MOBA_BLOCK // PAGE_SIZE
    npg = MOBA_TOPK * ppb
    ck, cv = refs[:npg], refs[npg:2 * npg]
    bl_ref, o_ref = refs[2 * npg:]
    b = pl.program_id(0)
    h = pl.program_id(1)
    scale = DB ** -0.5
    q = q_ref[...]
    q8 = jnp.broadcast_to(q, (8, DB))
    s_new = jnp.sum(q * kn_ref[...], axis=-1, keepdims=True) * scale + tab_ref[0, HA + h]
    far = tab_ref[N_BUCKETS - 1, HA + h]

    def compute(hs):
        pieces = []
        for s in range(npg):
            page = sel_ref[b * SEL_LANES + h * MOBA_TOPK + s // ppb] * ppb + s % ppb
            sc = _mm_nt(q8, ck[s][:, hs, :])[0:1, :] * scale
            pieces.append(sc + jnp.where(page == n_pages - 1, bl_ref[HA + hs, 0, 0:1, :], far))
        sc = jnp.concatenate(pieces, axis=1)
        m = jnp.maximum(jnp.max(sc, axis=-1, keepdims=True), s_new)
        e_new = jnp.exp(s_new - m)
        pe = jnp.exp(sc - m)
        acc = e_new * vn_ref[...]
        for s in range(npg):
            pe_s = jnp.broadcast_to(pe[:, s * PAGE_SIZE:(s + 1) * PAGE_SIZE], (8, PAGE_SIZE))
            acc = acc + _mm(pe_s, cv[s][:, hs, :])[0:1, :]
        o_ref[...] = acc / (e_new + jnp.sum(pe, axis=-1, keepdims=True))

    for hs in range(HB):
        pl.when(h == hs)(functools.partial(compute, hs))


def _decode_moba(page_table, sel, rel_bias, q, k_new, v_new, ckm, cvm, bias_last):
    db, n_pages = page_table.shape
    ppb = MOBA_BLOCK // PAGE_SIZE
    npg = MOBA_TOPK * ppb
    row = pl.BlockSpec((None, 1, DB), lambda b, h, pt, sl: (b, 0, h))

    def page_spec(s):
        def index(b, h, pt, sl):
            blk = sl[b * SEL_LANES + h * MOBA_TOPK + s // ppb]
            return (pt[b * n_pages + blk * ppb + s % ppb], 0, 0, 0)
        return pl.BlockSpec((None, PAGE_SIZE, HB, DB), index)

    pages = [page_spec(s) for s in range(npg)]
    grid_spec = pltpu.PrefetchScalarGridSpec(
        num_scalar_prefetch=2,
        grid=(db, HB),
        in_specs=[pl.BlockSpec(memory_space=pltpu.SMEM), row, row, row, *pages, *pages,
                  pl.BlockSpec(bias_last.shape, lambda b, h, pt, sl: (0, 0, 0, 0))],
        out_specs=row)
    return pl.pallas_call(
        functools.partial(_decode_moba_kernel, n_pages=n_pages),
        grid_spec=grid_spec,
        out_shape=jax.ShapeDtypeStruct((db, 1, B_W), F32),
        compiler_params=_params("parallel", "arbitrary"),
        name="decode_moba",
    )(page_table.reshape(-1), sel.reshape(-1), rel_bias, q, k_new, v_new, *([ckm] * npg), *([cvm] * npg), bias_last)


def _to_col(row):
    n = row.shape[1]
    eye = _iota2((n, n), 0) == _iota2((n, n), 1)
    return jnp.sum(jnp.where(eye, row, 0.0), axis=1, keepdims=True)


def _to_row(col):
    n = col.shape[0]
    eye = _iota2((n, n), 0) == _iota2((n, n), 1)
    return jnp.sum(jnp.where(eye, col, 0.0), axis=0, keepdims=True)


def _odd_step_kernel(cq_ref, cf_ref, ci_ref, cg_ref, r_ref, wl_ref, k_ref, v_ref, kk_ref, kka_ref, g_ref, bo_ref,
                     sh_ref, sr_ref, lb_ref, ng_ref, par_ref, o_ref, sh_out_ref, sr_out_ref):
    f = _hgrn_forget(cf_ref[...], lb_ref)
    q = _silu(cq_ref[...])
    k = 1.0 - f
    inp = ci_ref[...]
    outs = []
    for h in range(HC):
        cs = slice(h * DKC, (h + 1) * DKC)
        vs = slice(h * DVC, (h + 1) * DVC)
        s_new = sh_ref[h] * _to_col(f[:, cs]) + _to_col(k[:, cs]) * inp[:, vs]
        sh_out_ref[h] = s_new
        outs.append(_rms(jnp.sum(s_new * _to_col(q[:, cs]), axis=0, keepdims=True), ng_ref[...]))
    o_c = jnp.concatenate(outs, axis=1) * _silu(cg_ref[...])
    r = r_ref[...]
    w = jnp.exp(wl_ref[...])
    k2 = k_ref[...]
    v = v_ref[...]
    kk = kk_ref[...]
    kka = kka_ref[...]
    ys = []
    for h in range(HD):
        cs = slice(h * DD, (h + 1) * DD)
        s_old = sr_ref[h]
        sa = jnp.sum(s_old * (-kk[:, cs]), axis=1, keepdims=True)
        s_new = s_old * w[:, cs] + sa * kka[:, cs] + _to_col(v[:, cs]) * k2[:, cs]
        sr_out_ref[h] = s_new
        ys.append(_to_row(jnp.sum(s_new * r[:, cs], axis=1, keepdims=True)))
    y = jnp.broadcast_to(jnp.concatenate(ys, axis=1), (8, D_W))
    o_d = _rwkv_finish(y, bo_ref[...], g_ref[...], par_ref)[0:1, :]
    o_ref[...] = jnp.concatenate([o_c, o_d], axis=1)


def _odd_step(rows, s_hgrn, s_rwkv, hgrn_lb, norm_g, par):
    db = s_hgrn.shape[0]
    row = pl.BlockSpec((None, 1, D_W), lambda b: (b, 0, 0))
    sh = pl.BlockSpec((None, HC, DKC, DVC), lambda b: (b, 0, 0, 0))
    sr = pl.BlockSpec((None, HD, DD, DD), lambda b: (b, 0, 0, 0))
    return pl.pallas_call(
        _odd_step_kernel,
        grid=(db,),
        in_specs=[row] * 12 + [sh, sr,
                               pl.BlockSpec(hgrn_lb.shape, lambda b: (0, 0)),
                               pl.BlockSpec((1, DVC), lambda b: (0, 0)),
                               pl.BlockSpec(par.shape, lambda b: (0, 0))],
        out_specs=[pl.BlockSpec((None, 1, C_V + D_W), lambda b: (b, 0, 0)), sh, sr],
        out_shape=[jax.ShapeDtypeStruct((db, 1, C_V + D_W), F32),
                   jax.ShapeDtypeStruct(s_hgrn.shape, F32), jax.ShapeDtypeStruct(s_rwkv.shape, F32)],
        compiler_params=_params("parallel"),
        name="odd_step",
    )(*rows, s_hgrn, s_rwkv, hgrn_lb, norm_g.reshape(1, DVC), par)


TM = 512
FFN_TF = 1408
MOE_TF = 896


def kernel(x_prompt, x_sample, cache_k_diff, cache_v_diff, cache_k_moba, cache_v_moba, state_hgrn, state_rwkv, state_shift, page_table, ln_mix_0, w_in_0, lambda_q1, lambda_k1, lambda_q2, lambda_k2, subln_g, rel_bias, w_out_0, ln_ffn_0, ffn_w_gate, ffn_w_up, ffn_w_down, ln_mix_1, w_in_1, hgrn_lb, hgrn_norm_g, rwkv_mu, rwkv_w0, rwkv_w2, rwkv_a0, rwkv_a2, rwkv_g2, rwkv_k_k, rwkv_k_a, rwkv_r_k, rwkv_ln_w, rwkv_ln_b, w_out_1, ln_ffn_1, moe_router, moe_w_gate, moe_w_up, moe_w_down, ln_final):
    bp, tt, d = x_prompt.shape
    db = x_sample.shape[0]
    assert x_sample.shape[1] == 1 and d == D_MODEL
    n = bp * tt

    lam_vecs = jnp.stack([lambda_q1, lambda_k1, lambda_q2, lambda_k2]).astype(F32)
    bias = _bias_tiles(rel_bias, _prompt_bias_buckets())
    last_page_rel = PAGE_SIZE - np.arange(PAGE_SIZE)
    bias_last = _bias_tiles(rel_bias, np.broadcast_to(_t5_bucket_np(last_page_rel), (1, 8, PAGE_SIZE)).astype(np.int32))
    w1, mu1 = _odd_weight_layout(w_in_1, rwkv_mu)
    par = _rwkv_param_rows(rwkv_w0, rwkv_a0, rwkv_k_k, rwkv_k_a, rwkv_r_k, rwkv_ln_w, rwkv_ln_b)
    lw = _rwkv_lora_weights(rwkv_w2, rwkv_a2, rwkv_g2)
    even_splits = (A_QK, A_QK, A_V, B_W, B_W, B_W)

    xp = x_prompt.reshape(n, d)
    qa, ka, va, qb, kb, vb = _norm_matmul(xp, ln_mix_0, w_in_0, even_splits, (BF16, F32, F32, F32, F32, F32), TM)
    seq = lambda a: a.reshape(bp, tt, a.shape[-1])
    oa = _diff_attn_prompt(rel_bias, lam_vecs, seq(qa), seq(ka), seq(va), bias, subln_g)
    ob = _moba_prompt(rel_bias, seq(qb), seq(kb), seq(vb), bias)
    h = _matmul_residual(xp, oa.reshape(n, A_V), ob.reshape(n, B_W), w_out_0, TM)
    h = _ffn(h, ln_ffn_0, ffn_w_gate, ffn_w_up, ffn_w_down, TM, FFN_TF)
    cq, cf, ci, cg, r, k, v, lo, shift_p = _odd_in_proj(h, ln_mix_1, w1, mu1, TM, seq_len=tt)
    o_c, hgrn_p = _hgrn_prompt(seq(cq), seq(cf), seq(ci), seq(cg), hgrn_lb, hgrn_norm_g)
    wl, k2, kk, kka, g, bonus = _rwkv_prep(r, k, v, lo, par, lw, TM)
    o_d, h_state = _rwkv_prompt(seq(r), seq(wl), seq(k2), seq(v), seq(kk), seq(kka), seq(g), seq(bonus), par)
    rwkv_p = h_state.reshape(bp, HD, DD, DD).transpose(0, 1, 3, 2)
    h = _matmul_residual(h, o_c.reshape(n, C_V), o_d.reshape(n, D_W), w_out_1, TM)
    y_prompt = _moe_routed_final(h, ln_ffn_1, moe_router, moe_w_gate, moe_w_up, moe_w_down, ln_final, MOE_TF)

    xs = x_sample.reshape(db, d)
    qa_s, ka_s, va_s, qb_s, kb_s, vb_s = _norm_matmul(xs, ln_mix_0, w_in_0, even_splits, (F32,) * 6, TM)
    one = lambda a: a.reshape(db, 1, a.shape[-1])
    ckd_t = cache_k_diff.transpose(0, 2, 3, 4, 1)
    oa_s, page_sums = _decode_diff(page_table, rel_bias, lam_vecs, one(qa_s), one(ka_s), one(va_s),
                                   ckd_t, cache_v_diff, cache_k_moba, bias_last, subln_g)
    sel = _decode_gate(page_sums, one(qb_s))
    ob_s = _decode_moba(page_table, sel, rel_bias, one(qb_s), one(kb_s), one(vb_s), cache_k_moba, cache_v_moba,
                        bias_last)
    hs = _matmul_residual(xs, oa_s.reshape(db, A_V), ob_s.reshape(db, B_W), w_out_0, TM)
    hs = _ffn(hs, ln_ffn_0, ffn_w_gate, ffn_w_up, ffn_w_down, TM, FFN_TF)
    cq, cf, ci, cg, r, k, v, lo, shift_s = _odd_in_proj(hs, ln_mix_1, w1, mu1, TM, x_prev=state_shift)
    wl, k2, kk, kka, g, bonus = _rwkv_prep(r, k, v, lo, par, lw, TM)
    rows = [one(a) for a in (cq, cf, ci, cg, r, wl, k2, v, kk, kka, g, bonus)]
    o_cd, hgrn_s, rwkv_s = _odd_step(rows, state_hgrn, state_rwkv, hgrn_lb, hgrn_norm_g, par)
    o_cd = o_cd.reshape(db, C_V + D_W)
    hs = _matmul_residual(hs, o_cd[:, :C_V], o_cd[:, C_V:], w_out_1, TM)
    y_sample = _moe_final(hs, ln_ffn_1, moe_router, moe_w_gate, moe_w_up, moe_w_down, ln_final, TM, MOE_TF)

    return (y_prompt.reshape(bp, tt, d), y_sample.reshape(db, 1, d),
            ka.reshape(bp, tt, HA, 2, DA), va.reshape(bp, tt, HA, DVA),
            kb.reshape(bp, tt, HB, DB), vb.reshape(bp, tt, HB, DB),
            hgrn_p, rwkv_p, shift_p.reshape(bp, d),
            ka_s.reshape(db, 1, HA, 2, DA), va_s.reshape(db, 1, HA, DVA),
            kb_s.reshape(db, 1, HB, DB), vb_s.reshape(db, 1, HB, DB),
            hgrn_s, rwkv_s, shift_s)
```

```python
import functools
import math

import jax
import jax.numpy as jnp
import numpy as np
from jax import lax
from jax.experimental import pallas as pl
from jax.experimental.pallas import tpu as pltpu

F32 = jnp.float32
BF16 = jnp.bfloat16
MXU_DTYPE = jnp.bfloat16

D_MODEL = 1024
PAGE_SIZE = 128
HA, DA = 4, 64
DVA = 2 * DA
HB, DB = 4, 128
MOBA_BLOCK = 256
MOBA_TOPK = 3
N_BUCKETS = 32
MAX_DISTANCE = 128
HC, DKC, DVC = 4, 128, 128
HD, DD = 8, 64
D_DECAY_LORA, D_AAA_LORA, D_GATE_LORA = 32, 32, 96
RWKV_LN_EPS = 64e-5
N_EXPERTS = 8
EPS = 1e-6

A_QK = HA * 2 * DA
A_V = HA * DVA
B_W = HB * DB
C_K = HC * DKC
C_V = HC * DVC
HGRN_IN = 2 * C_K + 2 * C_V
D_W = HD * DD
RWKV_IN = 3 * D_W + D_DECAY_LORA + D_AAA_LORA + D_GATE_LORA

NEG = -1e30
VMEM_LIMIT_BYTES = 56 * 1024 * 1024
ATTN_TILE = 256


def _params(*sem):
    return pltpu.CompilerParams(dimension_semantics=sem, vmem_limit_bytes=VMEM_LIMIT_BYTES)


def _mm(a, b):
    return jnp.dot(a.astype(MXU_DTYPE), b.astype(MXU_DTYPE), preferred_element_type=F32)


def _mm_nt(a, b):
    return lax.dot_general(a.astype(MXU_DTYPE), b.astype(MXU_DTYPE), (((1,), (1,)), ((), ())),
                           preferred_element_type=F32)


def _rms(x, g):
    return x * lax.rsqrt(jnp.mean(x * x, axis=-1, keepdims=True) + EPS) * g


def _norm_matmul_kernel(x_ref, g_ref, w_ref, *out_refs, splits):
    y = _rms(x_ref[...], g_ref[...]).astype(MXU_DTYPE)
    off = 0
    for o_ref, width in zip(out_refs, splits):
        o_ref[...] = jnp.dot(y, w_ref[:, off:off + width], preferred_element_type=F32).astype(o_ref.dtype)
        off += width


def _norm_matmul(x, g, w, splits, dtypes, tm):
    n, d = x.shape
    tm = min(tm, n)
    assert n % tm == 0 and sum(splits) == w.shape[1]
    return pl.pallas_call(
        functools.partial(_norm_matmul_kernel, splits=tuple(splits)),
        grid=(n // tm,),
        in_specs=[pl.BlockSpec((tm, d), lambda i: (i, 0)),
                  pl.BlockSpec((1, d), lambda i: (0, 0)),
                  pl.BlockSpec(w.shape, lambda i: (0, 0))],
        out_specs=[pl.BlockSpec((tm, s), lambda i: (i, 0)) for s in splits],
        out_shape=[jax.ShapeDtypeStruct((n, s), dt) for s, dt in zip(splits, dtypes)],
        compiler_params=_params("parallel"),
        name="norm_matmul",
    )(x, g.reshape(1, d), w.astype(MXU_DTYPE))


def _even_in_proj_kernel(x_ref, g_ref, w_ref, wkt_ref, qa_ref, kat_ref, va_ref, qb_ref, kb_ref, vb_ref,
                         va4_ref, kb4_ref, vb4_ref):
    y = _rms(x_ref[...], g_ref[...]).astype(MXU_DTYPE)
    tm = y.shape[0]

    def proj(j):
        return jnp.dot(y, w_ref[:, j * A_QK:(j + 1) * A_QK], preferred_element_type=F32)

    qa_ref[...] = proj(0).astype(qa_ref.dtype)
    kat_ref[...] = _mm_nt(wkt_ref[...], y).reshape(HA, 2, DA, tm)
    qb_ref[...] = proj(3)
    for j, flat_ref, heads_ref in ((2, va_ref, va4_ref), (4, kb_ref, kb4_ref), (5, vb_ref, vb4_ref)):
        z = proj(j)
        flat_ref[...] = z
        for h in range(HA):
            heads_ref[:, h, :] = z[:, h * DVA:(h + 1) * DVA]


def _even_in_proj(x, g, w, seq_len, tm):
    n, d = x.shape
    assert n % seq_len == 0 and seq_len % tm == 0 and A_QK == A_V == B_W and HA == HB and DVA == DB
    nb, tps = n // seq_len, seq_len // tm
    flat = lambda dt: (jax.ShapeDtypeStruct((n, A_V), dt), pl.BlockSpec((tm, A_V), lambda i: (i, 0)))
    heads = (jax.ShapeDtypeStruct((nb, seq_len, HA, DVA), F32),
             pl.BlockSpec((None, tm, HA, DVA), lambda i: (i // tps, i % tps, 0, 0)))
    kat = (jax.ShapeDtypeStruct((nb, HA, 2, DA, seq_len), F32),
           pl.BlockSpec((None, HA, 2, DA, tm), lambda i: (i // tps, 0, 0, 0, i % tps)))
    outs = [flat(BF16), kat, flat(F32), flat(F32), flat(F32), flat(F32), heads, heads, heads]
    wk_t = w[:, A_QK:2 * A_QK].T.astype(MXU_DTYPE)
    return pl.pallas_call(
        _even_in_proj_kernel,
        grid=(n // tm,),
        in_specs=[pl.BlockSpec((tm, d), lambda i: (i, 0)),
                  pl.BlockSpec((1, d), lambda i: (0, 0)),
                  pl.BlockSpec(w.shape, lambda i: (0, 0)),
                  pl.BlockSpec(wk_t.shape, lambda i: (0, 0))],
        out_specs=[o[1] for o in outs],
        out_shape=[o[0] for o in outs],
        compiler_params=_params("parallel"),
        name="even_in_proj",
    )(x, g.reshape(1, d), w.astype(MXU_DTYPE), wk_t)


def _t5_bucket_np(rel):
    max_exact = N_BUCKETS // 2
    n = np.maximum(rel, 0)
    nf = np.maximum(n, max_exact).astype(np.float64)
    v = np.log(nf / max_exact) / math.log(MAX_DISTANCE / max_exact) * (N_BUCKETS - max_exact)
    frac = np.abs(v - np.round(v))
    assert np.all((frac > 1e-6) | (n <= max_exact) | (v >= N_BUCKETS - max_exact - 1e-6))
    large = max_exact + np.floor(v + 1e-9).astype(np.int64)
    return np.where(n < max_exact, n, np.minimum(large, N_BUCKETS - 1)).astype(np.int32)


def _bias_tiles_kernel(tab_ref, bk_ref, o_ref):
    h = pl.program_id(0)
    bk = bk_ref[...]
    acc = jnp.zeros(bk.shape, F32)
    for i in range(N_BUCKETS):
        acc = jnp.where(bk == i, tab_ref[i, h], acc)
    o_ref[0] = acc


def _bias_tiles(rel_bias, buckets):
    nh = rel_bias.shape[1]
    return pl.pallas_call(
        _bias_tiles_kernel,
        grid=(nh,),
        in_specs=[pl.BlockSpec(memory_space=pltpu.SMEM),
                  pl.BlockSpec(buckets.shape, lambda h: (0, 0, 0))],
        out_specs=pl.BlockSpec((1,) + buckets.shape, lambda h: (h, 0, 0, 0)),
        out_shape=jax.ShapeDtypeStruct((nh,) + buckets.shape, F32),
        compiler_params=_params("arbitrary"),
        name="bias_tiles",
    )(rel_bias, jnp.asarray(buckets))


def _softmax_pv(s, vb):
    p = jnp.exp(s - jnp.max(s, axis=-1, keepdims=True))
    return _mm(p, vb) / jnp.sum(p, axis=-1, keepdims=True)


def _causal(s):
    return jnp.where(_iota2(s.shape, 1) <= _iota2(s.shape, 0), s, NEG)


def _per_query_tile(qi, nq, compute):
    for qs in range(nq):
        pl.when(qi == qs)(functools.partial(compute, qs))


LAM_INIT_0 = 0.8 - 0.6 * math.exp(-0.3 * 0)


def _diff_lambda(lv_ref):
    s1 = jnp.sum(lv_ref[0:1, :] * lv_ref[1:2, :], axis=-1, keepdims=True)
    s2 = jnp.sum(lv_ref[2:3, :] * lv_ref[3:4, :], axis=-1, keepdims=True)
    return jnp.exp(s1) - jnp.exp(s2) + LAM_INIT_0


def _diff_attn_kernel(tab_ref, lv_ref, q_ref, k_ref, v_ref, bias_ref, g_ref, o_ref):
    h = pl.program_id(1)
    qi = pl.program_id(2)
    t = ATTN_TILE
    far = tab_ref[N_BUCKETS - 1, h]

    def compute(qs):
        w = (qs + 1) * t
        q = q_ref[...] * (DA ** -0.5)
        lane = _iota2(q.shape, 1)
        kt = k_ref[...].reshape(2 * DA, k_ref.shape[2])[:, 0:w].astype(MXU_DTYPE)
        vb = v_ref[0:w, :].astype(MXU_DTYPE)
        outs = []
        for c in range(2):
            qc = jnp.where((lane >= DA) == (c == 1), q, 0.0).astype(MXU_DTYPE)
            s = _mm(qc, kt)
            pieces = [_causal(s[:, qs * t:] + bias_ref[0])]
            if qs >= 1:
                pieces.insert(0, s[:, (qs - 1) * t:qs * t] + bias_ref[1])
            if qs >= 2:
                pieces.insert(0, s[:, :(qs - 1) * t] + far)
            outs.append(_softmax_pv(jnp.concatenate(pieces, axis=1), vb))
        o = outs[0] - _diff_lambda(lv_ref) * outs[1]
        o_ref[...] = (_rms(o, g_ref[...]) * (1.0 - LAM_INIT_0)).astype(o_ref.dtype)

    _per_query_tile(qi, v_ref.shape[0] // t, compute)


def _diff_attn_prompt(rel_bias, lam_vecs, q, k_t, v, bias, subln_g):
    b, tt, _ = q.shape
    t = ATTN_TILE
    assert tt % t == 0
    return pl.pallas_call(
        _diff_attn_kernel,
        grid=(b, HA, tt // t),
        in_specs=[pl.BlockSpec(memory_space=pltpu.SMEM),
                  pl.BlockSpec((4, DA), lambda bi, h, qi: (0, 0)),
                  pl.BlockSpec((None, t, DVA), lambda bi, h, qi: (bi, qi, h)),
                  pl.BlockSpec((None, None, 2, DA, tt), lambda bi, h, qi: (bi, h, 0, 0, 0)),
                  pl.BlockSpec((None, tt, DVA), lambda bi, h, qi: (bi, 0, h)),
                  pl.BlockSpec((None, 2, t, t), lambda bi, h, qi: (h, 0, 0, 0)),
                  pl.BlockSpec((1, DVA), lambda bi, h, qi: (0, 0))],
        out_specs=pl.BlockSpec((None, t, DVA), lambda bi, h, qi: (bi, qi, h)),
        out_shape=jax.ShapeDtypeStruct((b, tt, A_V), BF16),
        compiler_params=_params("parallel", "parallel", "arbitrary"),
        name="diff_attn_prompt",
    )(rel_bias, lam_vecs, q, k_t, v, bias, subln_g.reshape(1, DVA))


def _moba_select(gate_t, own):
    nb = gate_t.shape[0]
    blk = lax.broadcasted_iota(jnp.int32, gate_t.shape, 0)
    rank = jnp.zeros(gate_t.shape, F32)
    for m in range(nb):
        gm = gate_t[m:m + 1, :]
        beats = jnp.where(gm > gate_t, 1.0, jnp.where(gm == gate_t, jnp.where(m < blk, 1.0, 0.0), 0.0))
        rank = rank + jnp.where(m < own, beats, 0.0)
    return jnp.where(blk < own, jnp.where(rank < MOBA_TOPK, 1.0, 0.0), 0.0)


def _moba_kernel(tab_ref, q_ref, k_ref, v_ref, bias_ref, o_ref, means_ref, *, nb):
    h = pl.program_id(1)
    qi = pl.program_id(2)
    t = ATTN_TILE
    far = tab_ref[N_BUCKETS - 1, HA + h]

    @pl.when(qi == 0)
    def _():
        for j in range(nb):
            means_ref[j:j + 1, :] = jnp.sum(k_ref[j * t:(j + 1) * t, :], axis=0, keepdims=True) * (1.0 / MOBA_BLOCK)

    qf = q_ref[...]
    gate_t = lax.dot_general(means_ref[...], qf, (((1,), (1,)), ((), ())), precision=lax.Precision.HIGHEST,
                             preferred_element_type=F32)
    sel = _moba_select(gate_t, qi).T
    q = qf.astype(MXU_DTYPE)

    def compute(qs):
        w = (qs + 1) * t
        s = _mm_nt(q, k_ref[0:w, :]) * (DB ** -0.5)
        pieces = []
        for kj in range(qs):
            bias = bias_ref[1] if kj == qs - 1 else far
            pieces.append(jnp.where(sel[:, kj:kj + 1] > 0.5, s[:, kj * t:(kj + 1) * t] + bias, NEG))
        pieces.append(_causal(s[:, qs * t:] + bias_ref[0]))
        o_ref[...] = _softmax_pv(jnp.concatenate(pieces, axis=1), v_ref[0:w, :]).astype(o_ref.dtype)

    _per_query_tile(qi, nb, compute)


def _moba_prompt(rel_bias, q, k, v, bias):
    b, tt, _ = q.shape
    t = ATTN_TILE
    assert tt % t == 0 and t == MOBA_BLOCK
    nb = tt // t
    return pl.pallas_call(
        functools.partial(_moba_kernel, nb=nb),
        grid=(b, HB, nb),
        in_specs=[pl.BlockSpec(memory_space=pltpu.SMEM),
                  pl.BlockSpec((None, t, DB), lambda bi, h, qi: (bi, qi, h)),
                  pl.BlockSpec((None, tt, DB), lambda bi, h, qi: (bi, 0, h)),
                  pl.BlockSpec((None, tt, DB), lambda bi, h, qi: (bi, 0, h)),
                  pl.BlockSpec((None, 2, t, t), lambda bi, h, qi: (HA + h, 0, 0, 0))],
        out_specs=pl.BlockSpec((None, t, DB), lambda bi, h, qi: (bi, qi, h)),
        out_shape=jax.ShapeDtypeStruct((b, tt, B_W), BF16),
        scratch_shapes=[pltpu.VMEM((nb, DB), F32)],
        compiler_params=_params("parallel", "parallel", "arbitrary"),
        name="moba_prompt",
    )(rel_bias, q, k, v, bias)


def _prompt_bias_buckets():
    t = ATTN_TILE
    r = np.arange(t)[:, None]
    c = np.arange(t)[None, :]
    return np.stack([_t5_bucket_np(r - c), _t5_bucket_np(t + r - c)]).astype(np.int32)


def _matmul_residual_kernel(res_ref, a_ref, b_ref, w_ref, o_ref):
    ka = a_ref.shape[1]
    o_ref[...] = res_ref[...] + _mm(a_ref[...], w_ref[:ka, :]) + _mm(b_ref[...], w_ref[ka:, :])


def _matmul_residual(res, a, b, w, tm):
    n, d = res.shape
    tm = min(tm, n)
    assert n % tm == 0
    return pl.pallas_call(
        _matmul_residual_kernel,
        grid=(n // tm,),
        in_specs=[pl.BlockSpec((tm, d), lambda i: (i, 0)),
                  pl.BlockSpec((tm, a.shape[1]), lambda i: (i, 0)),
                  pl.BlockSpec((tm, b.shape[1]), lambda i: (i, 0)),
                  pl.BlockSpec(w.shape, lambda i: (0, 0))],
        out_specs=pl.BlockSpec((tm, d), lambda i: (i, 0)),
        out_shape=jax.ShapeDtypeStruct((n, d), F32),
        compiler_params=_params("parallel"),
        name="matmul_residual",
    )(res, a, b, w.astype(MXU_DTYPE))


def _silu(x):
    return x * jax.nn.sigmoid(x)


def _ffn_kernel(x_ref, g_ref, wg_ref, wu_ref, wd_ref, o_ref, xn_ref, acc_ref):
    f = pl.program_id(1)

    @pl.when(f == 0)
    def _():
        xn_ref[...] = _rms(x_ref[...], g_ref[...]).astype(xn_ref.dtype)
        acc_ref[...] = jnp.zeros(acc_ref.shape, F32)

    xn = xn_ref[...]
    a = _silu(_mm(xn, wg_ref[...])) * _mm(xn, wu_ref[...])
    acc_ref[...] += _mm(a, wd_ref[...])

    @pl.when(f == pl.num_programs(1) - 1)
    def _():
        o_ref[...] = x_ref[...] + acc_ref[...]


def _ffn(x, g, wg, wu, wd, tm, tf):
    n, d = x.shape
    ff = wg.shape[1]
    tm = min(tm, n)
    assert n % tm == 0 and ff % tf == 0
    return pl.pallas_call(
        _ffn_kernel,
        grid=(n // tm, ff // tf),
        in_specs=[pl.BlockSpec((tm, d), lambda i, f: (i, 0)),
                  pl.BlockSpec((1, d), lambda i, f: (0, 0)),
                  pl.BlockSpec((d, tf), lambda i, f: (0, f)),
                  pl.BlockSpec((d, tf), lambda i, f: (0, f)),
                  pl.BlockSpec((tf, d), lambda i, f: (f, 0))],
        out_specs=pl.BlockSpec((tm, d), lambda i, f: (i, 0)),
        out_shape=jax.ShapeDtypeStruct((n, d), F32),
        scratch_shapes=[pltpu.VMEM((tm, d), MXU_DTYPE), pltpu.VMEM((tm, d), F32)],
        compiler_params=_params("parallel", "arbitrary"),
        name="ffn",
    )(x, g.reshape(1, d), wg.astype(MXU_DTYPE), wu.astype(MXU_DTYPE), wd.astype(MXU_DTYPE))


def _router_gates(logits):
    ne = logits.shape[1]
    lane = lax.broadcasted_iota(jnp.int32, logits.shape, 1)
    m1 = jnp.max(logits, axis=-1, keepdims=True)
    i1 = jnp.min(jnp.where(logits == m1, lane, ne), axis=-1, keepdims=True)
    rest = jnp.where(lane == i1, -jnp.inf, logits)
    m2 = jnp.max(rest, axis=-1, keepdims=True)
    i2 = jnp.min(jnp.where(rest == m2, lane, ne), axis=-1, keepdims=True)
    e2 = jnp.exp(m2 - m1)
    den = 1.0 + e2
    return jnp.where(lane == i1, 1.0 / den, 0.0) + jnp.where(lane == i2, e2 / den, 0.0)


def _moe_dense_kernel(x_ref, g_ref, wr_ref, wg_ref, wu_ref, wd_ref, gf_ref, o_ref, xn_ref, gate_ref, acc_ref):
    e = pl.program_id(1)
    f = pl.program_id(2)

    @pl.when((e == 0) & (f == 0))
    def _():
        xn = _rms(x_ref[...], g_ref[...])
        xn_ref[...] = xn.astype(xn_ref.dtype)
        logits = jnp.dot(xn, wr_ref[...], precision=lax.Precision.HIGHEST, preferred_element_type=F32)
        gate_ref[...] = _router_gates(logits)
        acc_ref[...] = jnp.zeros(acc_ref.shape, F32)

    xn = xn_ref[...]
    a = _silu(_mm(xn, wg_ref[...])) * _mm(xn, wu_ref[...])
    lane = lax.broadcasted_iota(jnp.int32, gate_ref.shape, 1)
    gate_e = jnp.sum(jnp.where(lane == e, gate_ref[...], 0.0), axis=-1, keepdims=True)
    acc_ref[...] += _mm(a, wd_ref[...]) * gate_e

    @pl.when((e == pl.num_programs(1) - 1) & (f == pl.num_programs(2) - 1))
    def _():
        o_ref[...] = _rms(x_ref[...] + acc_ref[...], gf_ref[...])


def _moe_final(x, g, w_router, wg, wu, wd, g_final, tm, tf):
    n, d = x.shape
    ne, _, ff = wg.shape
    tm = min(tm, n)
    assert n % tm == 0 and ff % tf == 0
    return pl.pallas_call(
        _moe_dense_kernel,
        grid=(n // tm, ne, ff // tf),
        in_specs=[pl.BlockSpec((tm, d), lambda i, e, f: (i, 0)),
                  pl.BlockSpec((1, d), lambda i, e, f: (0, 0)),
                  pl.BlockSpec((d, ne), lambda i, e, f: (0, 0)),
                  pl.BlockSpec((None, d, tf), lambda i, e, f: (e, 0, f)),
                  pl.BlockSpec((None, d, tf), lambda i, e, f: (e, 0, f)),
                  pl.BlockSpec((None, tf, d), lambda i, e, f: (e, f, 0)),
                  pl.BlockSpec((1, d), lambda i, e, f: (0, 0))],
        out_specs=pl.BlockSpec((tm, d), lambda i, e, f: (i, 0)),
        out_shape=jax.ShapeDtypeStruct((n, d), F32),
        scratch_shapes=[pltpu.VMEM((tm, d), MXU_DTYPE), pltpu.VMEM((tm, ne), F32), pltpu.VMEM((tm, d), F32)],
        compiler_params=_params("parallel", "arbitrary", "arbitrary"),
        name="moe_final",
    )(x, g.reshape(1, d), w_router, wg.astype(MXU_DTYPE), wu.astype(MXU_DTYPE), wd.astype(MXU_DTYPE),
      g_final.reshape(1, d))


MOE_TM = 512
META_ROWS = 8


def _moe_route_kernel(x_ref, g_ref, wrt_ref, xn_ref, mi_ref, mw_ref, cnt_ref, carry_ref):
    i = pl.program_id(0)
    tm = x_ref.shape[0]
    ne = wrt_ref.shape[0]

    @pl.when(i == 0)
    def _():
        carry_ref[...] = jnp.zeros(carry_ref.shape, F32)

    xn = _rms(x_ref[...], g_ref[...])
    xn_ref[...] = xn
    logits = lax.dot_general(wrt_ref[...], xn, (((1,), (1,)), ((), ())), precision=lax.Precision.HIGHEST,
                             preferred_element_type=F32)
    sub = _iota2(logits.shape, 0)
    m1 = jnp.max(logits, axis=0, keepdims=True)
    i1 = jnp.min(jnp.where(logits == m1, sub, ne), axis=0, keepdims=True)
    rest = jnp.where(sub == i1, -jnp.inf, logits)
    m2 = jnp.max(rest, axis=0, keepdims=True)
    i2 = jnp.min(jnp.where(rest == m2, sub, ne), axis=0, keepdims=True)
    e2 = jnp.exp(m2 - m1)
    den = 1.0 + e2
    onehot = jnp.where(sub == i1, 1.0, jnp.where(sub == i2, 1.0, 0.0))
    before = jnp.where(_iota2((tm, tm), 0) < _iota2((tm, tm), 1), 1.0, 0.0)
    rank = jnp.dot(onehot.astype(BF16), before.astype(BF16), preferred_element_type=F32) + carry_ref[:, 0:1]
    r1 = jnp.sum(jnp.where(sub == i1, rank, 0.0), axis=0, keepdims=True)
    r2 = jnp.sum(jnp.where(sub == i2, rank, 0.0), axis=0, keepdims=True)
    zi = jnp.zeros((META_ROWS - 4, tm), jnp.int32)
    mi_ref[...] = jnp.concatenate([i1, i2, r1.astype(jnp.int32), r2.astype(jnp.int32), zi], axis=0)
    mw_ref[...] = jnp.concatenate([1.0 / den, e2 / den, jnp.zeros((META_ROWS - 2, tm), F32)], axis=0)
    carry_ref[...] = carry_ref[...] + jnp.sum(onehot, axis=1, keepdims=True)
    cnt_ref[...] = carry_ref[...]


def _moe_route(x, g, w_router):
    n, d = x.shape
    ne = w_router.shape[1]
    tm = MOE_TM
    assert n % tm == 0 and ne == META_ROWS
    meta = pl.BlockSpec((None, META_ROWS, tm), lambda i: (i, 0, 0))
    return pl.pallas_call(
        _moe_route_kernel,
        grid=(n // tm,),
        in_specs=[pl.BlockSpec((tm, d), lambda i: (i, 0)),
                  pl.BlockSpec((1, d), lambda i: (0, 0)),
                  pl.BlockSpec((ne, d), lambda i: (0, 0))],
        out_specs=[pl.BlockSpec((tm, d), lambda i: (i, 0)), meta, meta,
                   pl.BlockSpec((ne, 128), lambda i: (0, 0))],
        out_shape=[jax.ShapeDtypeStruct((n, d), F32),
                   jax.ShapeDtypeStruct((n // tm, META_ROWS, tm), jnp.int32),
                   jax.ShapeDtypeStruct((n // tm, META_ROWS, tm), F32),
                   jax.ShapeDtypeStruct((ne, 128), F32)],
        scratch_shapes=[pltpu.VMEM((ne, 128), F32)],
        compiler_params=_params("arbitrary"),
        name="moe_route",
    )(x, g.reshape(1, d), w_router.T)


def _row_copy(src, src_row, dst, dst_row, sem):
    return pltpu.make_async_copy(src.at[pl.ds(src_row, 1), :], dst.at[pl.ds(dst_row, 1), :], sem)


def _moe_slots_kernel(off_ref, mi_ref, sl_ref):
    rows = []
    for slot in range(2):
        e = mi_ref[slot:slot + 1, :]
        base = jnp.zeros(e.shape, jnp.int32)
        for x in range(META_ROWS):
            base = jnp.where(e == x, off_ref[x], base)
        rows.append(base + mi_ref[2 + slot:3 + slot, :])
    sl_ref[...] = jnp.concatenate(rows + [jnp.zeros((META_ROWS - 2, mi_ref.shape[1]), jnp.int32)], axis=0)


def _moe_slots(offsets, meta_i):
    nt, _, tm = meta_i.shape
    meta = pl.BlockSpec((None, META_ROWS, tm), lambda i, off: (i, 0, 0))
    return pl.pallas_call(
        _moe_slots_kernel,
        grid_spec=pltpu.PrefetchScalarGridSpec(num_scalar_prefetch=1, grid=(nt,), in_specs=[meta], out_specs=meta),
        out_shape=jax.ShapeDtypeStruct(meta_i.shape, jnp.int32),
        compiler_params=_params("parallel"),
        name="moe_slots",
    )(offsets, meta_i)


def _moe_dispatch_kernel(sl_ref, x_ref, xs_in_ref, xs_ref, sem):
    del xs_in_ref
    tm = x_ref.shape[0]

    def start(r, carry):
        for slot in range(2):
            _row_copy(x_ref, r, xs_ref, sl_ref[slot, r], sem).start()
        return carry

    lax.fori_loop(0, tm, start, 0, unroll=8)
    for slot in range(2):
        pltpu.make_async_copy(x_ref, xs_ref.at[pl.ds(0, tm), :], sem).wait()


def _moe_dispatch(slots, xn, n_sorted):
    n, d = xn.shape
    tm = MOE_TM
    return pl.pallas_call(
        _moe_dispatch_kernel,
        grid=(n // tm,),
        in_specs=[pl.BlockSpec((None, META_ROWS, tm), lambda i: (i, 0, 0), memory_space=pltpu.SMEM),
                  pl.BlockSpec((tm, d), lambda i: (i, 0)),
                  pl.BlockSpec(memory_space=pl.ANY)],
        out_specs=pl.BlockSpec(memory_space=pl.ANY),
        scratch_shapes=[pltpu.SemaphoreType.DMA(())],
        out_shape=jax.ShapeDtypeStruct((n_sorted, d), F32),
        input_output_aliases={2: 0},
        compiler_params=_params("arbitrary"),
        name="moe_dispatch",
    )(slots, xn, jnp.zeros((n_sorted, d), F32))


def _moe_experts_kernel(te_ref, nu_ref, x_ref, wg_ref, wu_ref, wd_ref, o_ref, xb_ref, acc_ref):
    t = pl.program_id(0)
    f = pl.program_id(1)

    @pl.when(t < nu_ref[0])
    def _():
        @pl.when(f == 0)
        def _():
            xb_ref[...] = x_ref[...].astype(xb_ref.dtype)
            acc_ref[...] = jnp.zeros(acc_ref.shape, F32)

        xb = xb_ref[...]
        a = _silu(_mm(xb, wg_ref[...])) * _mm(xb, wu_ref[...])
        acc_ref[...] += _mm(a, wd_ref[...])

        @pl.when(f == pl.num_programs(1) - 1)
        def _():
            o_ref[...] = acc_ref[...]

    @pl.when((t >= nu_ref[0]) & (f == pl.num_programs(1) - 1))
    def _():
        o_ref[...] = jnp.zeros(o_ref.shape, F32)


def _moe_experts(tile_expert, n_used, xs, wg, wu, wd, tf):
    r, d = xs.shape
    ne, _, ff = wg.shape
    tm = MOE_TM
    nf = ff // tf
    assert r % tm == 0 and ff % tf == 0

    def row_tile(t, f, te, nu):
        return (jnp.minimum(t, nu[0] - 1), 0)

    def fcol(t, f, nu):
        return jnp.where(t < nu[0], f, nf - 1)

    grid_spec = pltpu.PrefetchScalarGridSpec(
        num_scalar_prefetch=2,
        grid=(r // tm, nf),
        in_specs=[pl.BlockSpec((tm, d), row_tile),
                  pl.BlockSpec((None, d, tf), lambda t, f, te, nu: (te[t], 0, fcol(t, f, nu))),
                  pl.BlockSpec((None, d, tf), lambda t, f, te, nu: (te[t], 0, fcol(t, f, nu))),
                  pl.BlockSpec((None, tf, d), lambda t, f, te, nu: (te[t], fcol(t, f, nu), 0))],
        out_specs=pl.BlockSpec((tm, d), lambda t, f, te, nu: (t, 0)),
        scratch_shapes=[pltpu.VMEM((tm, d), MXU_DTYPE), pltpu.VMEM((tm, d), F32)])
    return pl.pallas_call(
        _moe_experts_kernel,
        grid_spec=grid_spec,
        out_shape=jax.ShapeDtypeStruct((r, d), F32),
        compiler_params=_params("arbitrary", "arbitrary"),
        name="moe_experts",
    )(tile_expert, n_used, xs, wg.astype(MXU_DTYPE), wu.astype(MXU_DTYPE), wd.astype(MXU_DTYPE))


def _moe_combine_kernel(sl_ref, mw_ref, x_ref, ys_ref, gf_ref, o_ref, buf_ref, sem):
    tm = x_ref.shape[0]

    def start(r, carry):
        for slot in range(2):
            _row_copy(ys_ref, sl_ref[slot, r], buf_ref.at[slot], r, sem).start()
        return carry

    lax.fori_loop(0, tm, start, 0, unroll=8)
    for slot in range(2):
        pltpu.make_async_copy(ys_ref.at[pl.ds(0, tm), :], buf_ref.at[slot], sem).wait()
    w1 = _to_col(mw_ref[0:1, :])
    w2 = _to_col(mw_ref[1:2, :])
    o_ref[...] = _rms(x_ref[...] + (buf_ref[0] * w1 + buf_ref[1] * w2), gf_ref[...])


def _moe_combine(slots, meta_w, x, ys, g_final):
    n, d = x.shape
    tm = MOE_TM
    meta = lambda space: pl.BlockSpec((None, META_ROWS, tm), lambda i: (i, 0, 0), memory_space=space)
    return pl.pallas_call(
        _moe_combine_kernel,
        grid=(n // tm,),
        in_specs=[meta(pltpu.SMEM), meta(pltpu.VMEM),
                  pl.BlockSpec((tm, d), lambda i: (i, 0)),
                  pl.BlockSpec(memory_space=pl.ANY),
                  pl.BlockSpec((1, d), lambda i: (0, 0))],
        out_specs=pl.BlockSpec((tm, d), lambda i: (i, 0)),
        scratch_shapes=[pltpu.VMEM((2, tm, d), F32), pltpu.SemaphoreType.DMA(())],
        out_shape=jax.ShapeDtypeStruct((n, d), F32),
        compiler_params=_params("arbitrary"),
        name="moe_combine",
    )(slots, meta_w, x, ys, g_final.reshape(1, d))


def _moe_routed_final(x, g, w_router, wg, wu, wd, g_final, tf):
    n, d = x.shape
    ne = w_router.shape[1]
    tm = MOE_TM
    xn, meta_i, meta_w, counts = _moe_route(x, g, w_router)
    counts = counts[:, 0].astype(jnp.int32)
    padded = (counts + tm - 1) // tm * tm
    ends = jnp.cumsum(padded)
    offsets = ends - padded
    n_tiles = (2 * n) // tm + ne
    n_used = (ends[-1] // tm).reshape(1)
    tile_start = jnp.arange(n_tiles, dtype=jnp.int32) * tm
    tile_expert = jnp.sum(tile_start[:, None] >= ends[None, :], axis=1).astype(jnp.int32)
    tile_expert = jnp.minimum(tile_expert, tile_expert[jnp.maximum(n_used[0] - 1, 0)])
    slots = _moe_slots(offsets, meta_i)
    xs = _moe_dispatch(slots, xn, n_tiles * tm)
    ys = _moe_experts(tile_expert, n_used, xs, wg, wu, wd, tf)
    return _moe_combine(slots, meta_w, x, ys, g_final)


ODD_RWKV_COLS = 3 * D_W + 256
ODD_SPLITS = (C_K, C_K, C_V, C_V, D_W, D_W, D_W, 256)


def _odd_weight_layout(w_in, mu):
    o = HGRN_IN
    r = slice(o, o + D_W)
    wd = slice(o + D_W, o + D_W + D_DECAY_LORA)
    k = slice(wd.stop, wd.stop + D_W)
    v = slice(k.stop, k.stop + D_W)
    rest = slice(v.stop, v.stop + D_AAA_LORA + D_GATE_LORA)
    pad = 256 - (D_DECAY_LORA + D_AAA_LORA + D_GATE_LORA)
    w = jnp.concatenate([w_in[:, :o], w_in[:, r], w_in[:, k], w_in[:, v], w_in[:, wd], w_in[:, rest],
                         jnp.zeros((w_in.shape[0], pad), w_in.dtype)], axis=1)
    ro = lambda s: slice(s.start - o, s.stop - o)
    m = jnp.concatenate([mu[ro(r)], mu[ro(k)], mu[ro(v)], mu[ro(wd)], mu[ro(rest)], jnp.zeros((pad,), mu.dtype)])
    return w, m


def _odd_in_proj_kernel(x_ref, g_ref, w_ref, mu_ref, *refs, shift, tiles_per_seq):
    if shift:
        (cq, cf, ci, cg, r, k, v, lo, xs_ref, carry_ref) = refs
        xn = _rms(x_ref[...], g_ref[...])
        xs_ref[...] = xn[xn.shape[0] - 1:, :]
    else:
        (xp_ref, cq, cf, ci, cg, r, k, v, lo, xs_ref) = refs
        xn = _rms(x_ref[...], g_ref[...])
        xs_ref[...] = xn
    y = xn.astype(MXU_DTYPE)
    off = 0
    for o_ref in (cq, cf, ci, cg):
        o_ref[...] = jnp.dot(y, w_ref[:, off:off + C_K], preferred_element_type=F32)
        off += C_K
    z = jnp.dot(y, w_ref[:, off:], preferred_element_type=F32)
    if shift:
        i = pl.program_id(0)

        @pl.when(i % tiles_per_seq == 0)
        def _():
            carry_ref[...] = jnp.zeros(carry_ref.shape, F32)

        row = lax.broadcasted_iota(jnp.int32, z.shape, 0)
        zp = jnp.where(row == 0, carry_ref[...], pltpu.roll(z, 1, 0))
        carry_ref[...] = z[z.shape[0] - 1:, :]
    else:
        zp = jnp.dot(xp_ref[...].astype(MXU_DTYPE), w_ref[:, off:], preferred_element_type=F32)
    z = z + (zp - z) * mu_ref[...]
    r[...] = z[:, :D_W]
    k[...] = z[:, D_W:2 * D_W]
    v[...] = z[:, 2 * D_W:3 * D_W]
    lo[...] = z[:, 3 * D_W:]


def _odd_in_proj(x, g, w, mu, tm, seq_len=None, x_prev=None):
    n, d = x.shape
    tm = min(tm, n)
    shift = x_prev is None
    assert n % tm == 0
    if shift:
        assert seq_len % tm == 0
    tiles_per_seq = seq_len // tm if shift else 1
    outs = [jax.ShapeDtypeStruct((n, s), F32) for s in ODD_SPLITS]
    out_specs = [pl.BlockSpec((tm, s), lambda i: (i, 0)) for s in ODD_SPLITS]
    in_specs = [pl.BlockSpec((tm, d), lambda i: (i, 0)),
                pl.BlockSpec((1, d), lambda i: (0, 0)),
                pl.BlockSpec(w.shape, lambda i: (0, 0)),
                pl.BlockSpec((1, ODD_RWKV_COLS), lambda i: (0, 0))]
    args = [x, g.reshape(1, d), w.astype(MXU_DTYPE), mu.reshape(1, ODD_RWKV_COLS)]
    scratch = []
    if shift:
        nseq = n // seq_len
        outs.append(jax.ShapeDtypeStruct((nseq, 1, d), F32))
        out_specs.append(pl.BlockSpec((None, 1, d), lambda i: (i // tiles_per_seq, 0, 0)))
        scratch.append(pltpu.VMEM((1, ODD_RWKV_COLS), F32))
    else:
        in_specs.append(pl.BlockSpec((tm, d), lambda i: (i, 0)))
        args.append(x_prev)
        outs.append(jax.ShapeDtypeStruct((n, d), F32))
        out_specs.append(pl.BlockSpec((tm, d), lambda i: (i, 0)))
    return pl.pallas_call(
        functools.partial(_odd_in_proj_kernel, shift=shift, tiles_per_seq=tiles_per_seq),
        grid=(n // tm,),
        in_specs=in_specs,
        out_specs=out_specs,
        out_shape=outs,
        scratch_shapes=scratch,
        compiler_params=_params("arbitrary"),
        name="odd_in_proj",
    )(*args)


def _split3(x):
    if MXU_DTYPE == F32:
        return (x,)
    x1 = x.astype(MXU_DTYPE)
    r1 = x - x1.astype(F32)
    x2 = r1.astype(MXU_DTYPE)
    x3 = (r1 - x2.astype(F32)).astype(MXU_DTYPE)
    return (x1, x2, x3)


def _mm_exact_lhs(ones, x):
    o = ones.astype(MXU_DTYPE)
    return sum(jnp.dot(o, p, preferred_element_type=F32) for p in _split3(x))


def _mm_exact_rhs(x, ones):
    o = ones.astype(MXU_DTYPE)
    return sum(jnp.dot(p, o, preferred_element_type=F32) for p in _split3(x))


def _iota2(shape, dim):
    return lax.broadcasted_iota(jnp.int32, shape, dim)


def _group_ones(n, width):
    return jnp.where(_iota2((n, n), 0) // width == _iota2((n, n), 1) // width, 1.0, 0.0)


HGRN_CHUNK = 32
REC_TILE = 256


def _hgrn_forget(cf, lb_ref):
    l0 = lb_ref[0:1, :]
    l1 = lb_ref[1:2, :]
    mx = jnp.maximum(l0, l1)
    e0 = jnp.exp(l0 - mx)
    e1 = jnp.exp(l1 - mx)
    w0 = e0 / (e0 + e1)
    w1 = e1 / (e0 + e1)
    lb = (w0 + w1) - w0
    return lb + (1.0 - lb) * jax.nn.sigmoid(cf)


def _hgrn_kernel(cq_ref, cf_ref, ci_ref, cg_ref, lb_ref, g_ref, o_ref, s_out_ref, s_ref):
    si = pl.program_id(1)
    t = REC_TILE
    c = HGRN_CHUNK
    nc = t // c

    @pl.when(si == 0)
    def _():
        s_ref[...] = jnp.zeros(s_ref.shape, F32)

    f = _hgrn_forget(cf_ref[...], lb_ref)
    q = _silu(cq_ref[...])
    k = 1.0 - f
    logf = jnp.log(f)
    inp = ci_ref[...]
    row = _iota2((t, t), 0)
    col = _iota2((t, t), 1)
    same = row // c == col // c
    cum_m = jnp.where(same & (col <= row), 1.0, 0.0)
    ref_m = jnp.where(same & (col % c <= c // 2), 1.0, 0.0)
    last_m = jnp.where(same, 1.0, 0.0)
    b = _mm_exact_lhs(cum_m, logf)
    b_ref = _mm_exact_lhs(ref_m, logf)
    b_last = _mm_exact_lhs(last_m, logf)
    sel8 = jnp.where(_iota2((nc, t), 0) == _iota2((nc, t), 1) // c, 1.0, 0.0)
    dec_t = jnp.exp(_mm_exact_lhs(sel8, logf)).T
    qs = (q * jnp.exp(b - b_ref)).astype(MXU_DTYPE)
    ks = (k * jnp.exp(b_ref - b)).astype(MXU_DTYPE)
    q_inter = (q * jnp.exp(b)).astype(MXU_DTYPE)
    k_state = (k * jnp.exp(b_last - b)).astype(MXU_DTYPE)
    inp_m = inp.astype(MXU_DTYPE)
    causal = same & (col <= row)
    outs = []
    for h in range(HC):
        cs = slice(h * DKC, (h + 1) * DKC)
        vs = slice(h * DVC, (h + 1) * DVC)
        scores = jnp.where(causal, _mm_nt(qs[:, cs], ks[:, cs]), 0.0)
        o_h = _mm(scores, inp_m[:, vs])
        state = s_ref[h]
        inter = []
        for j in range(nc):
            rs = slice(j * c, (j + 1) * c)
            inter.append(_mm(q_inter[rs, cs], state))
            upd = lax.dot_general(k_state[rs, cs], inp_m[rs, vs], (((0,), (0,)), ((), ())),
                                  preferred_element_type=F32)
            state = state * dec_t[cs, j:j + 1] + upd
        s_ref[h] = state
        o_h = o_h + jnp.concatenate(inter, axis=0)
        outs.append(_rms(o_h, g_ref[...]))
    o = jnp.concatenate(outs, axis=1) * _silu(cg_ref[...])
    o_ref[...] = o.astype(o_ref.dtype)

    @pl.when(si == pl.num_programs(1) - 1)
    def _():
        s_out_ref[...] = s_ref[...]


def _hgrn_prompt(cq, cf, ci, cg, hgrn_lb, norm_g):
    b, tt, _ = cq.shape
    t = REC_TILE
    assert tt % t == 0
    blk = pl.BlockSpec((None, t, C_K), lambda bi, si: (bi, si, 0))
    return pl.pallas_call(
        _hgrn_kernel,
        grid=(b, tt // t),
        in_specs=[blk, blk, blk, blk,
                  pl.BlockSpec(hgrn_lb.shape, lambda bi, si: (0, 0)),
                  pl.BlockSpec((1, DVC), lambda bi, si: (0, 0))],
        out_specs=[pl.BlockSpec((None, t, C_V), lambda bi, si: (bi, si, 0)),
                   pl.BlockSpec((None, HC, DKC, DVC), lambda bi, si: (bi, 0, 0, 0))],
        out_shape=[jax.ShapeDtypeStruct((b, tt, C_V), BF16), jax.ShapeDtypeStruct((b, HC, DKC, DVC), F32)],
        scratch_shapes=[pltpu.VMEM((HC, DKC, DVC), F32)],
        compiler_params=_params("parallel", "arbitrary"),
        name="hgrn_prompt",
    )(cq, cf, ci, cg, hgrn_lb, norm_g.reshape(1, DVC))


RWKV_CHUNK = 32
LORA_COLS = 256


def _rwkv_param_rows(w0, a0, k_k, k_a, r_k, ln_w, ln_b):
    return jnp.stack([w0, a0, k_k, k_a, r_k.reshape(-1), ln_w, ln_b, jnp.zeros_like(w0)]).astype(F32)


def _rwkv_lora_weights(w2, a2, g2):
    z = jnp.zeros((3, LORA_COLS, D_W), F32)
    z = z.at[0, :D_DECAY_LORA].set(w2)
    z = z.at[1, D_DECAY_LORA:D_DECAY_LORA + D_AAA_LORA].set(a2)
    o = D_DECAY_LORA + D_AAA_LORA
    return z.at[2, o:o + D_GATE_LORA].set(g2)


def _dot_hi(a, b):
    return jnp.dot(a, b, precision=lax.Precision.HIGHEST, preferred_element_type=F32)


def _rwkv_activations(r, k, v, lo, par_ref, lw_ref):
    w0, a0, k_k, k_a, r_k = (par_ref[i:i + 1, :] for i in range(5))
    x = -(w0 + _dot_hi(jnp.tanh(lo), lw_ref[0]))
    softplus = jnp.maximum(x, 0.0) + jnp.log(1.0 + jnp.exp(-jnp.abs(x)))
    wlog = -jnp.exp(-softplus - 0.5)
    a = jax.nn.sigmoid(a0 + _dot_hi(lo, lw_ref[1]))
    g = _dot_hi(jax.nn.sigmoid(lo), lw_ref[2])
    ones = _group_ones(D_W, DD)
    kk = k * k_k
    norm = jnp.sqrt(_mm_exact_rhs(kk * kk, ones))
    kk = kk / jnp.maximum(norm, 1e-12)
    k2 = k * (1.0 + (a - 1.0) * k_a)
    bonus = _mm_exact_rhs(r * k2 * r_k, ones) * v
    return wlog, k2, kk, kk * a, g, bonus


def _rwkv_finish(y, bonus, g, par_ref):
    ln_w = par_ref[5:6, :]
    ln_b = par_ref[6:7, :]
    ones = _group_ones(D_W, DD)
    mu = _mm_exact_rhs(y, ones) * (1.0 / DD)
    d = y - mu
    var = _mm_exact_rhs(d * d, ones) * (1.0 / DD)
    return (d * lax.rsqrt(var + RWKV_LN_EPS) * ln_w + ln_b + bonus) * g


def _rwkv_prep_kernel(r_ref, k_ref, v_ref, lo_ref, par_ref, lw_ref, wl_ref, k2_ref, kk_ref, kka_ref, g_ref, bo_ref):
    outs = _rwkv_activations(r_ref[...], k_ref[...], v_ref[...], lo_ref[...], par_ref, lw_ref)
    for o_ref, val in zip((wl_ref, k2_ref, kk_ref, kka_ref, g_ref, bo_ref), outs):
        o_ref[...] = val


def _rwkv_prep(r, k, v, lo, par, lw, tm):
    n = r.shape[0]
    tm = min(tm, n)
    assert n % tm == 0
    blk = pl.BlockSpec((tm, D_W), lambda i: (i, 0))
    return pl.pallas_call(
        _rwkv_prep_kernel,
        grid=(n // tm,),
        in_specs=[blk, blk, blk, pl.BlockSpec((tm, LORA_COLS), lambda i: (i, 0)),
                  pl.BlockSpec(par.shape, lambda i: (0, 0)),
                  pl.BlockSpec(lw.shape, lambda i: (0, 0, 0))],
        out_specs=[blk] * 6,
        out_shape=[jax.ShapeDtypeStruct((n, D_W), F32)] * 6,
        compiler_params=_params("parallel"),
        name="rwkv_prep",
    )(r, k, v, lo, par, lw)


def _pack_heads(x):
    return jnp.concatenate([x[:, h * DD:(h + 1) * DD] for h in range(HD)], axis=0)


def _unpack_heads(x):
    c = x.shape[0] // HD
    return jnp.concatenate([x[h * c:(h + 1) * c, :] for h in range(HD)], axis=1)


def _rwkv_masks(c):
    n = HD * c
    row = _iota2((n, n), 0)
    col = _iota2((n, n), 1)
    same = row // c == col // c
    keep = _iota2((n, HD * DD), 0) // c == _iota2((n, HD * DD), 1) // DD
    tri = jnp.where(_iota2((c, c), 1) <= _iota2((c, c), 0), 1.0, 0.0)
    return same & (col < row), same & (col <= row), jnp.where(row == col, 1.0, 0.0), keep, tri


def _tile_heads(x, keep):
    return jnp.where(keep, jnp.concatenate([x] * HD, axis=0), 0.0).astype(MXU_DTYPE)


def _rwkv_chunk(r, wl, k, v, al, be, ht, masks):
    c = r.shape[0]
    strict, incl, eye, keep, tri = masks
    cum = _mm_exact_lhs(tri, wl)
    e_neg = jnp.exp(-cum)
    r_hat = r * jnp.exp(cum)
    a_hat = al * jnp.exp(cum - wl)
    k_til = k * e_neg
    b_til = be * e_neg
    pr, pa, pk, pb = (_pack_heads(x).astype(MXU_DTYPE) for x in (r_hat, a_hat, k_til, b_til))
    pv_t = _pack_heads(v).T.astype(MXU_DTYPE)
    nmat = jnp.where(strict, _mm_nt(pa, pb), 0.0)
    a_k = jnp.where(strict, _mm_nt(pa, pk), 0.0)
    r_k = jnp.where(incl, _mm_nt(pr, pk), 0.0)
    r_b = jnp.where(incl, _mm_nt(pr, pb), 0.0)
    inv = eye - nmat
    pw = nmat
    steps = int(math.log2(c))
    assert 2 ** steps == c
    for _ in range(steps - 1):
        pw = _mm(pw, pw)
        inv = inv + _mm(inv, pw)
    w_t = _mm_nt(ht, _tile_heads(a_hat, keep)) + _mm_nt(pv_t, a_k)
    u_t = _mm_nt(w_t, inv)
    y_t = _mm_nt(ht, _tile_heads(r_hat, keep)) + _mm_nt(pv_t, r_k) - _mm_nt(u_t, r_b)
    gam = jnp.exp(cum[c - 1:c, :])
    ht_new = gam * (ht + _mm(pv_t, _tile_heads(k_til, keep)) - _mm(u_t, _tile_heads(b_til, keep)))
    return _unpack_heads(y_t.T), ht_new


RWKV_SEQS = 2


def _rwkv_kernel(r_ref, wl_ref, k_ref, v_ref, al_ref, be_ref, g_ref, bo_ref, par_ref, o_ref, s_out_ref, h_ref, y_ref):
    si = pl.program_id(1)
    c = RWKV_CHUNK
    nseq = r_ref.shape[0]

    @pl.when(si == 0)
    def _():
        h_ref[...] = jnp.zeros(h_ref.shape, F32)

    masks = _rwkv_masks(c)

    states = [h_ref[s] for s in range(nseq)]
    for j in range(REC_TILE // c):
        rows = slice(j * c, (j + 1) * c)
        for s in range(nseq):
            y, states[s] = _rwkv_chunk(r_ref[s, rows, :], wl_ref[s, rows, :], k_ref[s, rows, :], v_ref[s, rows, :],
                                       al_ref[s, rows, :], be_ref[s, rows, :], states[s], masks)
            y_ref[s, rows, :] = y
    for s in range(nseq):
        h_ref[s] = states[s]
    for s in range(nseq):
        o_ref[s] = _rwkv_finish(y_ref[s], bo_ref[s], g_ref[s], par_ref).astype(o_ref.dtype)

    @pl.when(si == pl.num_programs(1) - 1)
    def _():
        s_out_ref[...] = h_ref[...]


def _rwkv_prompt(r, wl, k, v, al, be, g, bonus, par):
    b, tt, _ = r.shape
    t = REC_TILE
    ns = RWKV_SEQS if b % RWKV_SEQS == 0 else 1
    assert tt % t == 0
    blk = pl.BlockSpec((ns, t, D_W), lambda bi, si: (bi, si, 0))
    return pl.pallas_call(
        _rwkv_kernel,
        grid=(b // ns, tt // t),
        in_specs=[blk] * 8 + [pl.BlockSpec(par.shape, lambda bi, si: (0, 0))],
        out_specs=[blk, pl.BlockSpec((ns, DD, HD * DD), lambda bi, si: (bi, 0, 0))],
        out_shape=[jax.ShapeDtypeStruct((b, tt, D_W), BF16), jax.ShapeDtypeStruct((b, DD, HD * DD), F32)],
        scratch_shapes=[pltpu.VMEM((ns, DD, HD * DD), F32), pltpu.VMEM((ns, t, D_W), F32)],
        compiler_params=_params("parallel", "arbitrary"),
        name="rwkv_prompt",
    )(r, wl, k, v, al, be, g, bonus, par)


def _head_rows(bl_ref, first, count, reps):
    rows = []
    for i in range(count):
        rows += [bl_ref[first + i, 0, 0:1, :]] * reps
    return jnp.concatenate(rows, axis=0) if len(rows) > 1 else rows[0]


DECODE_PAGES = 8


def _decode_diff_kernel(pt_ref, tab_ref, lv_ref, q_ref, kn_ref, vn_ref, *refs):
    pp = DECODE_PAGES
    ck, cv, cm = refs[:pp], refs[pp:2 * pp], refs[2 * pp:3 * pp]
    bl_ref, g_ref, o_ref, sums_ref, m_ref, l_ref, acc_ref, qm_ref = refs[3 * pp:]
    p = pl.program_id(1)
    last = pl.num_programs(1) - 1
    nr = 2 * HA
    rowc = _iota2((nr, 1), 0)

    def head_col(bucket):
        col = jnp.zeros((nr, 1), F32)
        for h in range(HA):
            col = jnp.where(rowc // 2 == h, tab_ref[bucket, h], col)
        return col

    @pl.when(p == 0)
    def _():
        q = q_ref[...] * (DA ** -0.5)
        keep = _iota2((nr, A_QK), 1) // DA == _iota2((nr, A_QK), 0)
        qm = jnp.where(keep, jnp.broadcast_to(q, (nr, A_QK)), 0.0)
        qm_ref[...] = qm
        m_ref[...] = jnp.sum(qm * kn_ref[...], axis=-1, keepdims=True) + head_col(0)
        l_ref[...] = jnp.ones(l_ref.shape, F32)
        for h in range(HA):
            acc_ref[h] = jnp.broadcast_to(vn_ref[:, h * DVA:(h + 1) * DVA], (nr, DVA))

    qm = qm_ref[...].astype(MXU_DTYPE)
    s = jnp.concatenate([jnp.dot(qm, ck[j][...].reshape(A_QK, PAGE_SIZE).astype(MXU_DTYPE),
                                 preferred_element_type=F32) for j in range(pp)], axis=1)
    far = head_col(N_BUCKETS - 1)
    near = jnp.concatenate([jnp.broadcast_to(far, (nr, (pp - 1) * PAGE_SIZE)), _head_rows(bl_ref, 0, HA, 2)], axis=1)
    s = s + jnp.where(p == last, near, far)
    m_prev = m_ref[...]
    m_new = jnp.maximum(m_prev, jnp.max(s, axis=-1, keepdims=True))
    alpha = jnp.exp(m_prev - m_new)
    pe = jnp.exp(s - m_new)
    l_ref[...] = alpha * l_ref[...] + jnp.sum(pe, axis=-1, keepdims=True)
    for h in range(HA):
        vh = jnp.concatenate([cv[j][:, h, :] for j in range(pp)], axis=0)
        acc_ref[h] = alpha * acc_ref[h] + _mm(pe, vh)
    m_ref[...] = m_new
    for j in range(pp):
        sums_ref[j] = jnp.sum(cm[j][...], axis=0)

    @pl.when(p == last)
    def _():
        lam = _diff_lambda(lv_ref)
        pieces = []
        for h in range(HA):
            outs = acc_ref[h] / l_ref[...]
            o_h = outs[2 * h:2 * h + 1, :] - lam * outs[2 * h + 1:2 * h + 2, :]
            pieces.append(_rms(o_h, g_ref[...]) * (1.0 - LAM_INIT_0))
        o_ref[...] = jnp.concatenate(pieces, axis=1)


def _decode_diff(page_table, rel_bias, lam_vecs, q, k_new, v_new, ckd_t, cvd, ckm, bias_last, subln_g):
    db, n_pages = page_table.shape
    pp = DECODE_PAGES
    assert n_pages % pp == 0
    row = pl.BlockSpec((None, 1, A_V), lambda b, p, pt: (b, 0, 0))

    def page_spec(block, j):
        zeros = (0,) * len(block)
        return pl.BlockSpec((None,) + block, lambda b, p, pt: (pt[b * n_pages + p * pp + j],) + zeros)

    kt_pages = [page_spec((HA, 2, DA, PAGE_SIZE), j) for j in range(pp)]
    v_pages = [page_spec((PAGE_SIZE, HA, DVA), j) for j in range(pp)]
    m_pages = [page_spec((PAGE_SIZE, HB, DB), j) for j in range(pp)]
    grid_spec = pltpu.PrefetchScalarGridSpec(
        num_scalar_prefetch=1,
        grid=(db, n_pages // pp),
        in_specs=[pl.BlockSpec(memory_space=pltpu.SMEM),
                  pl.BlockSpec((4, DA), lambda b, p, pt: (0, 0)),
                  row, row, row, *kt_pages, *v_pages, *m_pages,
                  pl.BlockSpec(bias_last.shape, lambda b, p, pt: (0, 0, 0, 0)),
                  pl.BlockSpec((1, DVA), lambda b, p, pt: (0, 0))],
        out_specs=[row, pl.BlockSpec((None, pp, HB, DB), lambda b, p, pt: (b, p, 0, 0))],
        scratch_shapes=[pltpu.VMEM((2 * HA, 1), F32), pltpu.VMEM((2 * HA, 1), F32),
                        pltpu.VMEM((HA, 2 * HA, DVA), F32), pltpu.VMEM((2 * HA, A_QK), F32)])
    return pl.pallas_call(
        _decode_diff_kernel,
        grid_spec=grid_spec,
        out_shape=[jax.ShapeDtypeStruct((db, 1, A_V), F32), jax.ShapeDtypeStruct((db, n_pages, HB, DB), F32)],
        compiler_params=_params("parallel", "arbitrary"),
        name="decode_diff",
    )(page_table.reshape(-1), rel_bias, lam_vecs, q, k_new, v_new, *([ckd_t] * pp), *([cvd] * pp), *([ckm] * pp),
      bias_last, subln_g.reshape(1, DVA))


SEL_LANES = 128


def _decode_gate_kernel(sums_ref, q_ref, sel_ref):
    n_pages = sums_ref.shape[0]
    ppb = MOBA_BLOCK // PAGE_SIZE
    nb = n_pages // ppb
    pair = jnp.where(_iota2((nb, n_pages), 1) // ppb == _iota2((nb, n_pages), 0), 1.0, 0.0)
    blk = _iota2((nb, 1), 0)
    lane = _iota2((1, SEL_LANES), 1)
    out = jnp.zeros((1, SEL_LANES), jnp.int32)
    for h in range(HB):
        means = _mm_exact_lhs(pair, sums_ref[:, h, :]) * (1.0 / MOBA_BLOCK)
        gate = jnp.sum(means * q_ref[:, h * DB:(h + 1) * DB], axis=-1, keepdims=True)
        for slot in range(MOBA_TOPK):
            best = jnp.max(gate, axis=0, keepdims=True)
            idx = jnp.min(jnp.where(gate == best, blk, nb), axis=0, keepdims=True)
            out = jnp.where(lane == h * MOBA_TOPK + slot, idx, out)
            gate = jnp.where(blk == idx, -jnp.inf, gate)
    sel_ref[...] = out


def _decode_gate(sums, q):
    db, n_pages = sums.shape[:2]
    assert n_pages % (MOBA_BLOCK // PAGE_SIZE) == 0 and n_pages * PAGE_SIZE // MOBA_BLOCK >= MOBA_TOPK
    return pl.pallas_call(
        _decode_gate_kernel,
        grid=(db,),
        in_specs=[pl.BlockSpec((None, n_pages, HB, DB), lambda b: (b, 0, 0, 0)),
                  pl.BlockSpec((None, 1, B_W), lambda b: (b, 0, 0))],
        out_specs=pl.BlockSpec((None, 1, SEL_LANES), lambda b: (b, 0, 0)),
        out_shape=jax.ShapeDtypeStruct((db, 1, SEL_LANES), jnp.int32),
        compiler_params=_params("parallel"),
        name="decode_gate",
    )(sums, q)


def _decode_moba_kernel(pt_ref, sel_ref, tab_ref, q_ref, kn_ref, vn_ref, *refs, n_pages):
    ppb = MOBA_BLOCK // PAGE_SIZE
    npg = MOBA_TOPK * ppb
    ck, cv = refs[:npg], refs[npg:2 * npg]
    bl_ref, o_ref = refs[2 * npg:]
    b = pl.program_id(0)
    h = pl.program_id(1)
    scale = DB ** -0.5
    q = q_ref[...]
    q8 = jnp.broadcast_to(q, (8, DB))
    s_new = jnp.sum(q * kn_ref[...], axis=-1, keepdims=True) * scale + tab_ref[0, HA + h]
    far = tab_ref[N_BUCKETS - 1, HA + h]

    def compute(hs):
        pieces = []
        for s in range(npg):
            page = sel_ref[b * SEL_LANES + h * MOBA_TOPK + s // ppb] * ppb + s % ppb
            sc = _mm_nt(q8, ck[s][:, hs, :])[0:1, :] * scale
            pieces.append(sc + jnp.where(page == n_pages - 1, bl_ref[HA + hs, 0, 0:1, :], far))
        sc = jnp.concatenate(pieces, axis=1)
        m = jnp.maximum(jnp.max(sc, axis=-1, keepdims=True), s_new)
        e_new = jnp.exp(s_new - m)
        pe = jnp.exp(sc - m)
        acc = e_new * vn_ref[...]
        for s in range(npg):
            pe_s = jnp.broadcast_to(pe[:, s * PAGE_SIZE:(s + 1) * PAGE_SIZE], (8, PAGE_SIZE))
            acc = acc + _mm(pe_s, cv[s][:, hs, :])[0:1, :]
        o_ref[...] = acc / (e_new + jnp.sum(pe, axis=-1, keepdims=True))

    for hs in range(HB):
        pl.when(h == hs)(functools.partial(compute, hs))


def _decode_moba(page_table, sel, rel_bias, q, k_new, v_new, ckm, cvm, bias_last):
    db, n_pages = page_table.shape
    ppb = MOBA_BLOCK // PAGE_SIZE
    npg = MOBA_TOPK * ppb
    row = pl.BlockSpec((None, 1, DB), lambda b, h, pt, sl: (b, 0, h))

    def page_spec(s):
        def index(b, h, pt, sl):
            blk = sl[b * SEL_LANES + h * MOBA_TOPK + s // ppb]
            return (pt[b * n_pages + blk * ppb + s % ppb], 0, 0, 0)
        return pl.BlockSpec((None, PAGE_SIZE, HB, DB), index)

    pages = [page_spec(s) for s in range(npg)]
    grid_spec = pltpu.PrefetchScalarGridSpec(
        num_scalar_prefetch=2,
        grid=(db, HB),
        in_specs=[pl.BlockSpec(memory_space=pltpu.SMEM), row, row, row, *pages, *pages,
                  pl.BlockSpec(bias_last.shape, lambda b, h, pt, sl: (0, 0, 0, 0))],
        out_specs=row)
    return pl.pallas_call(
        functools.partial(_decode_moba_kernel, n_pages=n_pages),
        grid_spec=grid_spec,
        out_shape=jax.ShapeDtypeStruct((db, 1, B_W), F32),
        compiler_params=_params("parallel", "arbitrary"),
        name="decode_moba",
    )(page_table.reshape(-1), sel.reshape(-1), rel_bias, q, k_new, v_new, *([ckm] * npg), *([cvm] * npg), bias_last)


def _to_col(row):
    n = row.shape[1]
    eye = _iota2((n, n), 0) == _iota2((n, n), 1)
    return jnp.sum(jnp.where(eye, row, 0.0), axis=1, keepdims=True)


def _to_row(col):
    n = col.shape[0]
    eye = _iota2((n, n), 0) == _iota2((n, n), 1)
    return jnp.sum(jnp.where(eye, col, 0.0), axis=0, keepdims=True)


def _odd_step_kernel(cq_ref, cf_ref, ci_ref, cg_ref, r_ref, wl_ref, k_ref, v_ref, kk_ref, kka_ref, g_ref, bo_ref,
                     sh_ref, sr_ref, lb_ref, ng_ref, par_ref, o_ref, sh_out_ref, sr_out_ref):
    f = _hgrn_forget(cf_ref[...], lb_ref)
    q = _silu(cq_ref[...])
    k = 1.0 - f
    inp = ci_ref[...]
    outs = []
    for h in range(HC):
        cs = slice(h * DKC, (h + 1) * DKC)
        vs = slice(h * DVC, (h + 1) * DVC)
        s_new = sh_ref[h] * _to_col(f[:, cs]) + _to_col(k[:, cs]) * inp[:, vs]
        sh_out_ref[h] = s_new
        outs.append(_rms(jnp.sum(s_new * _to_col(q[:, cs]), axis=0, keepdims=True), ng_ref[...]))
    o_c = jnp.concatenate(outs, axis=1) * _silu(cg_ref[...])
    r = r_ref[...]
    w = jnp.exp(wl_ref[...])
    k2 = k_ref[...]
    v = v_ref[...]
    kk = kk_ref[...]
    kka = kka_ref[...]
    ys = []
    for h in range(HD):
        cs = slice(h * DD, (h + 1) * DD)
        s_old = sr_ref[h]
        sa = jnp.sum(s_old * (-kk[:, cs]), axis=1, keepdims=True)
        s_new = s_old * w[:, cs] + sa * kka[:, cs] + _to_col(v[:, cs]) * k2[:, cs]
        sr_out_ref[h] = s_new
        ys.append(_to_row(jnp.sum(s_new * r[:, cs], axis=1, keepdims=True)))
    y = jnp.broadcast_to(jnp.concatenate(ys, axis=1), (8, D_W))
    o_d = _rwkv_finish(y, bo_ref[...], g_ref[...], par_ref)[0:1, :]
    o_ref[...] = jnp.concatenate([o_c, o_d], axis=1)


def _odd_step(rows, s_hgrn, s_rwkv, hgrn_lb, norm_g, par):
    db = s_hgrn.shape[0]
    row = pl.BlockSpec((None, 1, D_W), lambda b: (b, 0, 0))
    sh = pl.BlockSpec((None, HC, DKC, DVC), lambda b: (b, 0, 0, 0))
    sr = pl.BlockSpec((None, HD, DD, DD), lambda b: (b, 0, 0, 0))
    return pl.pallas_call(
        _odd_step_kernel,
        grid=(db,),
        in_specs=[row] * 12 + [sh, sr,
                               pl.BlockSpec(hgrn_lb.shape, lambda b: (0, 0)),
                               pl.BlockSpec((1, DVC), lambda b: (0, 0)),
                               pl.BlockSpec(par.shape, lambda b: (0, 0))],
        out_specs=[pl.BlockSpec((None, 1, C_V + D_W), lambda b: (b, 0, 0)), sh, sr],
        out_shape=[jax.ShapeDtypeStruct((db, 1, C_V + D_W), F32),
                   jax.ShapeDtypeStruct(s_hgrn.shape, F32), jax.ShapeDtypeStruct(s_rwkv.shape, F32)],
        compiler_params=_params("parallel"),
        name="odd_step",
    )(*rows, s_hgrn, s_rwkv, hgrn_lb, norm_g.reshape(1, DVC), par)


TM = 512
FFN_TF = 1408
MOE_TF = 896


def kernel(x_prompt, x_sample, cache_k_diff, cache_v_diff, cache_k_moba, cache_v_moba, state_hgrn, state_rwkv, state_shift, page_table, ln_mix_0, w_in_0, lambda_q1, lambda_k1, lambda_q2, lambda_k2, subln_g, rel_bias, w_out_0, ln_ffn_0, ffn_w_gate, ffn_w_up, ffn_w_down, ln_mix_1, w_in_1, hgrn_lb, hgrn_norm_g, rwkv_mu, rwkv_w0, rwkv_w2, rwkv_a0, rwkv_a2, rwkv_g2, rwkv_k_k, rwkv_k_a, rwkv_r_k, rwkv_ln_w, rwkv_ln_b, w_out_1, ln_ffn_1, moe_router, moe_w_gate, moe_w_up, moe_w_down, ln_final):
    bp, tt, d = x_prompt.shape
    db = x_sample.shape[0]
    assert x_sample.shape[1] == 1 and d == D_MODEL
    n = bp * tt

    lam_vecs = jnp.stack([lambda_q1, lambda_k1, lambda_q2, lambda_k2]).astype(F32)
    bias = _bias_tiles(rel_bias, _prompt_bias_buckets())
    last_page_rel = PAGE_SIZE - np.arange(PAGE_SIZE)
    bias_last = _bias_tiles(rel_bias, np.broadcast_to(_t5_bucket_np(last_page_rel), (1, 8, PAGE_SIZE)).astype(np.int32))
    w1, mu1 = _odd_weight_layout(w_in_1, rwkv_mu)
    par = _rwkv_param_rows(rwkv_w0, rwkv_a0, rwkv_k_k, rwkv_k_a, rwkv_r_k, rwkv_ln_w, rwkv_ln_b)
    lw = _rwkv_lora_weights(rwkv_w2, rwkv_a2, rwkv_g2)
    even_splits = (A_QK, A_QK, A_V, B_W, B_W, B_W)

    xp = x_prompt.reshape(n, d)
    qa, ka_t, va, qb, kb, vb, va4, kb4, vb4 = _even_in_proj(xp, ln_mix_0, w_in_0, tt, TM)
    seq = lambda a: a.reshape(bp, tt, a.shape[-1])
    oa = _diff_attn_prompt(rel_bias, lam_vecs, seq(qa), ka_t, seq(va), bias, subln_g)
    ob = _moba_prompt(rel_bias, seq(qb), seq(kb), seq(vb), bias)
    h = _matmul_residual(xp, oa.reshape(n, A_V), ob.reshape(n, B_W), w_out_0, TM)
    h = _ffn(h, ln_ffn_0, ffn_w_gate, ffn_w_up, ffn_w_down, TM, FFN_TF)
    cq, cf, ci, cg, r, k, v, lo, shift_p = _odd_in_proj(h, ln_mix_1, w1, mu1, TM, seq_len=tt)
    o_c, hgrn_p = _hgrn_prompt(seq(cq), seq(cf), seq(ci), seq(cg), hgrn_lb, hgrn_norm_g)
    wl, k2, kk, kka, g, bonus = _rwkv_prep(r, k, v, lo, par, lw, TM)
    o_d, h_state = _rwkv_prompt(seq(r), seq(wl), seq(k2), seq(v), seq(kk), seq(kka), seq(g), seq(bonus), par)
    rwkv_p = h_state.reshape(bp, DD, HD, DD).transpose(0, 2, 1, 3)
    h = _matmul_residual(h, o_c.reshape(n, C_V), o_d.reshape(n, D_W), w_out_1, TM)
    y_prompt = _moe_routed_final(h, ln_ffn_1, moe_router, moe_w_gate, moe_w_up, moe_w_down, ln_final, MOE_TF)

    xs = x_sample.reshape(db, d)
    qa_s, ka_s, va_s, qb_s, kb_s, vb_s = _norm_matmul(xs, ln_mix_0, w_in_0, even_splits, (F32,) * 6, TM)
    one = lambda a: a.reshape(db, 1, a.shape[-1])
    ckd_t = cache_k_diff.transpose(0, 2, 3, 4, 1)
    oa_s, page_sums = _decode_diff(page_table, rel_bias, lam_vecs, one(qa_s), one(ka_s), one(va_s),
                                   ckd_t, cache_v_diff, cache_k_moba, bias_last, subln_g)
    sel = _decode_gate(page_sums, one(qb_s))
    ob_s = _decode_moba(page_table, sel, rel_bias, one(qb_s), one(kb_s), one(vb_s), cache_k_moba, cache_v_moba,
                        bias_last)
    hs = _matmul_residual(xs, oa_s.reshape(db, A_V), ob_s.reshape(db, B_W), w_out_0, TM)
    hs = _ffn(hs, ln_ffn_0, ffn_w_gate, ffn_w_up, ffn_w_down, TM, FFN_TF)
    cq, cf, ci, cg, r, k, v, lo, shift_s = _odd_in_proj(hs, ln_mix_1, w1, mu1, TM, x_prev=state_shift)
    wl, k2, kk, kka, g, bonus = _rwkv_prep(r, k, v, lo, par, lw, TM)
    rows = [one(a) for a in (cq, cf, ci, cg, r, wl, k2, v, kk, kka, g, bonus)]
    o_cd, hgrn_s, rwkv_s = _odd_step(rows, state_hgrn, state_rwkv, hgrn_lb, hgrn_norm_g, par)
    o_cd = o_cd.reshape(db, C_V + D_W)
    hs = _matmul_residual(hs, o_cd[:, :C_V], o_cd[:, C_V:], w_out_1, TM)
    y_sample = _moe_final(hs, ln_ffn_1, moe_router, moe_w_gate, moe_w_up, moe_w_down, ln_final, TM, MOE_TF)

    return (y_prompt.reshape(bp, tt, d), y_sample.reshape(db, 1, d),
            ka_t.transpose(0, 4, 1, 2, 3), va4, kb4, vb4,
            hgrn_p, rwkv_p, shift_p.reshape(bp, d),
            ka_s.reshape(db, 1, HA, 2, DA), va_s.reshape(db, 1, HA, DVA),
            kb_s.reshape(db, 1, HB, DB), vb_s.reshape(db, 1, HB, DB),
            hgrn_s, rwkv_s, shift_s)
```

```python
import functools
import math

import jax
import jax.numpy as jnp
import numpy as np
from jax import lax
from jax.experimental import pallas as pl
from jax.experimental.pallas import tpu as pltpu

F32 = jnp.float32
BF16 = jnp.bfloat16
MXU_DTYPE = jnp.bfloat16

D_MODEL = 1024
PAGE_SIZE = 128
HA, DA = 4, 64
DVA = 2 * DA
HB, DB = 4, 128
MOBA_BLOCK = 256
MOBA_TOPK = 3
N_BUCKETS = 32
MAX_DISTANCE = 128
HC, DKC, DVC = 4, 128, 128
HD, DD = 8, 64
D_DECAY_LORA, D_AAA_LORA, D_GATE_LORA = 32, 32, 96
RWKV_LN_EPS = 64e-5
N_EXPERTS = 8
EPS = 1e-6

A_QK = HA * 2 * DA
A_V = HA * DVA
B_W = HB * DB
C_K = HC * DKC
C_V = HC * DVC
HGRN_IN = 2 * C_K + 2 * C_V
D_W = HD * DD
RWKV_IN = 3 * D_W + D_DECAY_LORA + D_AAA_LORA + D_GATE_LORA

NEG = -1e30
VMEM_LIMIT_BYTES = 56 * 1024 * 1024
ATTN_TILE = 256


def _params(*sem):
    return pltpu.CompilerParams(dimension_semantics=sem, vmem_limit_bytes=VMEM_LIMIT_BYTES)


def _mm(a, b):
    return jnp.dot(a.astype(MXU_DTYPE), b.astype(MXU_DTYPE), preferred_element_type=F32)


def _mm_nt(a, b):
    return lax.dot_general(a.astype(MXU_DTYPE), b.astype(MXU_DTYPE), (((1,), (1,)), ((), ())),
                           preferred_element_type=F32)


def _rms(x, g):
    return x * lax.rsqrt(jnp.mean(x * x, axis=-1, keepdims=True) + EPS) * g


def _norm_matmul_kernel(x_ref, g_ref, w_ref, *out_refs, splits):
    y = _rms(x_ref[...], g_ref[...]).astype(MXU_DTYPE)
    off = 0
    for o_ref, width in zip(out_refs, splits):
        o_ref[...] = jnp.dot(y, w_ref[:, off:off + width], preferred_element_type=F32).astype(o_ref.dtype)
        off += width


def _norm_matmul(x, g, w, splits, dtypes, tm):
    n, d = x.shape
    tm = min(tm, n)
    assert n % tm == 0 and sum(splits) == w.shape[1]
    return pl.pallas_call(
        functools.partial(_norm_matmul_kernel, splits=tuple(splits)),
        grid=(n // tm,),
        in_specs=[pl.BlockSpec((tm, d), lambda i: (i, 0)),
                  pl.BlockSpec((1, d), lambda i: (0, 0)),
                  pl.BlockSpec(w.shape, lambda i: (0, 0))],
        out_specs=[pl.BlockSpec((tm, s), lambda i: (i, 0)) for s in splits],
        out_shape=[jax.ShapeDtypeStruct((n, s), dt) for s, dt in zip(splits, dtypes)],
        compiler_params=_params("parallel"),
        name="norm_matmul",
    )(x, g.reshape(1, d), w.astype(MXU_DTYPE))


def _even_in_proj_kernel(x_ref, g_ref, w_ref, wkt_ref, qa_ref, kat_ref, va_ref, qb_ref, kb_ref, vb_ref,
                         va4_ref, kb4_ref, vb4_ref):
    y = _rms(x_ref[...], g_ref[...]).astype(MXU_DTYPE)
    tm = y.shape[0]

    def proj(j):
        return jnp.dot(y, w_ref[:, j * A_QK:(j + 1) * A_QK], preferred_element_type=F32)

    qa_ref[...] = proj(0).astype(qa_ref.dtype)
    kat_ref[...] = _mm_nt(wkt_ref[...], y).reshape(HA, 2, DA, tm)
    qb_ref[...] = proj(3)
    for j, flat_ref, heads_ref in ((2, va_ref, va4_ref), (4, kb_ref, kb4_ref), (5, vb_ref, vb4_ref)):
        z = proj(j)
        flat_ref[...] = z
        for h in range(HA):
            heads_ref[:, h, :] = z[:, h * DVA:(h + 1) * DVA]


def _even_in_proj(x, g, w, seq_len, tm):
    n, d = x.shape
    assert n % seq_len == 0 and seq_len % tm == 0 and A_QK == A_V == B_W and HA == HB and DVA == DB
    nb, tps = n // seq_len, seq_len // tm
    flat = lambda dt: (jax.ShapeDtypeStruct((n, A_V), dt), pl.BlockSpec((tm, A_V), lambda i: (i, 0)))
    heads = (jax.ShapeDtypeStruct((nb, seq_len, HA, DVA), F32),
             pl.BlockSpec((None, tm, HA, DVA), lambda i: (i // tps, i % tps, 0, 0)))
    kat = (jax.ShapeDtypeStruct((nb, HA, 2, DA, seq_len), F32),
           pl.BlockSpec((None, HA, 2, DA, tm), lambda i: (i // tps, 0, 0, 0, i % tps)))
    outs = [flat(BF16), kat, flat(F32), flat(F32), flat(F32), flat(F32), heads, heads, heads]
    wk_t = w[:, A_QK:2 * A_QK].T.astype(MXU_DTYPE)
    return pl.pallas_call(
        _even_in_proj_kernel,
        grid=(n // tm,),
        in_specs=[pl.BlockSpec((tm, d), lambda i: (i, 0)),
                  pl.BlockSpec((1, d), lambda i: (0, 0)),
                  pl.BlockSpec(w.shape, lambda i: (0, 0)),
                  pl.BlockSpec(wk_t.shape, lambda i: (0, 0))],
        out_specs=[o[1] for o in outs],
        out_shape=[o[0] for o in outs],
        compiler_params=_params("parallel"),
        name="even_in_proj",
    )(x, g.reshape(1, d), w.astype(MXU_DTYPE), wk_t)


def _t5_bucket_np(rel):
    max_exact = N_BUCKETS // 2
    n = np.maximum(rel, 0)
    nf = np.maximum(n, max_exact).astype(np.float64)
    v = np.log(nf / max_exact) / math.log(MAX_DISTANCE / max_exact) * (N_BUCKETS - max_exact)
    frac = np.abs(v - np.round(v))
    assert np.all((frac > 1e-6) | (n <= max_exact) | (v >= N_BUCKETS - max_exact - 1e-6))
    large = max_exact + np.floor(v + 1e-9).astype(np.int64)
    return np.where(n < max_exact, n, np.minimum(large, N_BUCKETS - 1)).astype(np.int32)


def _bias_tiles_kernel(tab_ref, bk_ref, o_ref):
    h = pl.program_id(0)
    bk = bk_ref[...]
    acc = jnp.zeros(bk.shape, F32)
    for i in range(N_BUCKETS):
        acc = jnp.where(bk == i, tab_ref[i, h], acc)
    o_ref[0] = acc


def _bias_tiles(rel_bias, buckets):
    nh = rel_bias.shape[1]
    return pl.pallas_call(
        _bias_tiles_kernel,
        grid=(nh,),
        in_specs=[pl.BlockSpec(memory_space=pltpu.SMEM),
                  pl.BlockSpec(buckets.shape, lambda h: (0, 0, 0))],
        out_specs=pl.BlockSpec((1,) + buckets.shape, lambda h: (h, 0, 0, 0)),
        out_shape=jax.ShapeDtypeStruct((nh,) + buckets.shape, F32),
        compiler_params=_params("arbitrary"),
        name="bias_tiles",
    )(rel_bias, jnp.asarray(buckets))


def _softmax_pv(s, vb):
    p = jnp.exp(s - jnp.max(s, axis=-1, keepdims=True))
    return _mm(p, vb) / jnp.sum(p, axis=-1, keepdims=True)


def _causal(s):
    return jnp.where(_iota2(s.shape, 1) <= _iota2(s.shape, 0), s, NEG)


def _per_query_tile(qi, nq, compute):
    for qs in range(nq):
        pl.when(qi == qs)(functools.partial(compute, qs))


LAM_INIT_0 = 0.8 - 0.6 * math.exp(-0.3 * 0)


def _diff_lambda(lv_ref):
    s1 = jnp.sum(lv_ref[0:1, :] * lv_ref[1:2, :], axis=-1, keepdims=True)
    s2 = jnp.sum(lv_ref[2:3, :] * lv_ref[3:4, :], axis=-1, keepdims=True)
    return jnp.exp(s1) - jnp.exp(s2) + LAM_INIT_0


def _diff_attn_kernel(tab_ref, lv_ref, q_ref, k_ref, v_ref, bias_ref, g_ref, o_ref):
    h = pl.program_id(1)
    qi = pl.program_id(2)
    t = ATTN_TILE
    far = tab_ref[N_BUCKETS - 1, h]

    def compute(qs):
        w = (qs + 1) * t
        q = q_ref[...] * (DA ** -0.5)
        lane = _iota2(q.shape, 1)
        kt = k_ref[...].reshape(2 * DA, k_ref.shape[2])[:, 0:w].astype(MXU_DTYPE)
        vb = v_ref[0:w, :].astype(MXU_DTYPE)
        outs = []
        for c in range(2):
            qc = jnp.where((lane >= DA) == (c == 1), q, 0.0).astype(MXU_DTYPE)
            s = _mm(qc, kt)
            pieces = [_causal(s[:, qs * t:] + bias_ref[0])]
            if qs >= 1:
                pieces.insert(0, s[:, (qs - 1) * t:qs * t] + bias_ref[1])
            if qs >= 2:
                pieces.insert(0, s[:, :(qs - 1) * t] + far)
            outs.append(_softmax_pv(jnp.concatenate(pieces, axis=1), vb))
        o = outs[0] - _diff_lambda(lv_ref) * outs[1]
        o_ref[...] = (_rms(o, g_ref[...]) * (1.0 - LAM_INIT_0)).astype(o_ref.dtype)

    _per_query_tile(qi, v_ref.shape[0] // t, compute)


def _diff_attn_prompt(rel_bias, lam_vecs, q, k_t, v, bias, subln_g):
    b, tt, _ = q.shape
    t = ATTN_TILE
    assert tt % t == 0
    return pl.pallas_call(
        _diff_attn_kernel,
        grid=(b, HA, tt // t),
        in_specs=[pl.BlockSpec(memory_space=pltpu.SMEM),
                  pl.BlockSpec((4, DA), lambda bi, h, qi: (0, 0)),
                  pl.BlockSpec((None, t, DVA), lambda bi, h, qi: (bi, qi, h)),
                  pl.BlockSpec((None, None, 2, DA, tt), lambda bi, h, qi: (bi, h, 0, 0, 0)),
                  pl.BlockSpec((None, tt, DVA), lambda bi, h, qi: (bi, 0, h)),
                  pl.BlockSpec((None, 2, t, t), lambda bi, h, qi: (h, 0, 0, 0)),
                  pl.BlockSpec((1, DVA), lambda bi, h, qi: (0, 0))],
        out_specs=pl.BlockSpec((None, t, DVA), lambda bi, h, qi: (bi, qi, h)),
        out_shape=jax.ShapeDtypeStruct((b, tt, A_V), BF16),
        compiler_params=_params("parallel", "parallel", "arbitrary"),
        name="diff_attn_prompt",
    )(rel_bias, lam_vecs, q, k_t, v, bias, subln_g.reshape(1, DVA))


def _moba_select(gate_t, own):
    nb = gate_t.shape[0]
    blk = lax.broadcasted_iota(jnp.int32, gate_t.shape, 0)
    rank = jnp.zeros(gate_t.shape, F32)
    for m in range(nb):
        gm = gate_t[m:m + 1, :]
        beats = jnp.where(gm > gate_t, 1.0, jnp.where(gm == gate_t, jnp.where(m < blk, 1.0, 0.0), 0.0))
        rank = rank + jnp.where(m < own, beats, 0.0)
    return jnp.where(blk < own, jnp.where(rank < MOBA_TOPK, 1.0, 0.0), 0.0)


def _moba_kernel(tab_ref, q_ref, k_ref, v_ref, bias_ref, o_ref, means_ref, *, nb):
    h = pl.program_id(1)
    qi = pl.program_id(2)
    t = ATTN_TILE
    far = tab_ref[N_BUCKETS - 1, HA + h]

    @pl.when(qi == 0)
    def _():
        for j in range(nb):
            means_ref[j:j + 1, :] = jnp.sum(k_ref[j * t:(j + 1) * t, :], axis=0, keepdims=True) * (1.0 / MOBA_BLOCK)

    qf = q_ref[...]
    gate_t = lax.dot_general(means_ref[...], qf, (((1,), (1,)), ((), ())), precision=lax.Precision.HIGHEST,
                             preferred_element_type=F32)
    sel = _moba_select(gate_t, qi).T
    q = qf.astype(MXU_DTYPE)

    def compute(qs):
        w = (qs + 1) * t
        s = _mm_nt(q, k_ref[0:w, :]) * (DB ** -0.5)
        pieces = []
        for kj in range(qs):
            bias = bias_ref[1] if kj == qs - 1 else far
            pieces.append(jnp.where(sel[:, kj:kj + 1] > 0.5, s[:, kj * t:(kj + 1) * t] + bias, NEG))
        pieces.append(_causal(s[:, qs * t:] + bias_ref[0]))
        o_ref[...] = _softmax_pv(jnp.concatenate(pieces, axis=1), v_ref[0:w, :]).astype(o_ref.dtype)

    _per_query_tile(qi, nb, compute)


def _moba_prompt(rel_bias, q, k, v, bias):
    b, tt, _ = q.shape
    t = ATTN_TILE
    assert tt % t == 0 and t == MOBA_BLOCK
    nb = tt // t
    return pl.pallas_call(
        functools.partial(_moba_kernel, nb=nb),
        grid=(b, HB, nb),
        in_specs=[pl.BlockSpec(memory_space=pltpu.SMEM),
                  pl.BlockSpec((None, t, DB), lambda bi, h, qi: (bi, qi, h)),
                  pl.BlockSpec((None, tt, DB), lambda bi, h, qi: (bi, 0, h)),
                  pl.BlockSpec((None, tt, DB), lambda bi, h, qi: (bi, 0, h)),
                  pl.BlockSpec((None, 2, t, t), lambda bi, h, qi: (HA + h, 0, 0, 0))],
        out_specs=pl.BlockSpec((None, t, DB), lambda bi, h, qi: (bi, qi, h)),
        out_shape=jax.ShapeDtypeStruct((b, tt, B_W), BF16),
        scratch_shapes=[pltpu.VMEM((nb, DB), F32)],
        compiler_params=_params("parallel", "parallel", "arbitrary"),
        name="moba_prompt",
    )(rel_bias, q, k, v, bias)


def _prompt_bias_buckets():
    t = ATTN_TILE
    r = np.arange(t)[:, None]
    c = np.arange(t)[None, :]
    return np.stack([_t5_bucket_np(r - c), _t5_bucket_np(t + r - c)]).astype(np.int32)


def _matmul_residual_kernel(res_ref, a_ref, b_ref, w_ref, o_ref):
    ka = a_ref.shape[1]
    o_ref[...] = res_ref[...] + _mm(a_ref[...], w_ref[:ka, :]) + _mm(b_ref[...], w_ref[ka:, :])


def _matmul_residual(res, a, b, w, tm):
    n, d = res.shape
    tm = min(tm, n)
    assert n % tm == 0
    return pl.pallas_call(
        _matmul_residual_kernel,
        grid=(n // tm,),
        in_specs=[pl.BlockSpec((tm, d), lambda i: (i, 0)),
                  pl.BlockSpec((tm, a.shape[1]), lambda i: (i, 0)),
                  pl.BlockSpec((tm, b.shape[1]), lambda i: (i, 0)),
                  pl.BlockSpec(w.shape, lambda i: (0, 0))],
        out_specs=pl.BlockSpec((tm, d), lambda i: (i, 0)),
        out_shape=jax.ShapeDtypeStruct((n, d), F32),
        compiler_params=_params("parallel"),
        name="matmul_residual",
    )(res, a, b, w.astype(MXU_DTYPE))


def _silu(x):
    return x * jax.nn.sigmoid(x)


def _ffn_kernel(x_ref, g_ref, wg_ref, wu_ref, wd_ref, o_ref, xn_ref, acc_ref):
    f = pl.program_id(1)

    @pl.when(f == 0)
    def _():
        xn_ref[...] = _rms(x_ref[...], g_ref[...]).astype(xn_ref.dtype)
        acc_ref[...] = jnp.zeros(acc_ref.shape, F32)

    xn = xn_ref[...]
    a = _silu(_mm(xn, wg_ref[...])) * _mm(xn, wu_ref[...])
    acc_ref[...] += _mm(a, wd_ref[...])

    @pl.when(f == pl.num_programs(1) - 1)
    def _():
        o_ref[...] = x_ref[...] + acc_ref[...]


def _ffn(x, g, wg, wu, wd, tm, tf):
    n, d = x.shape
    ff = wg.shape[1]
    tm = min(tm, n)
    assert n % tm == 0 and ff % tf == 0
    return pl.pallas_call(
        _ffn_kernel,
        grid=(n // tm, ff // tf),
        in_specs=[pl.BlockSpec((tm, d), lambda i, f: (i, 0)),
                  pl.BlockSpec((1, d), lambda i, f: (0, 0)),
                  pl.BlockSpec((d, tf), lambda i, f: (0, f)),
                  pl.BlockSpec((d, tf), lambda i, f: (0, f)),
                  pl.BlockSpec((tf, d), lambda i, f: (f, 0))],
        out_specs=pl.BlockSpec((tm, d), lambda i, f: (i, 0)),
        out_shape=jax.ShapeDtypeStruct((n, d), F32),
        scratch_shapes=[pltpu.VMEM((tm, d), MXU_DTYPE), pltpu.VMEM((tm, d), F32)],
        compiler_params=_params("parallel", "arbitrary"),
        name="ffn",
    )(x, g.reshape(1, d), wg.astype(MXU_DTYPE), wu.astype(MXU_DTYPE), wd.astype(MXU_DTYPE))


def _router_gates(logits):
    ne = logits.shape[1]
    lane = lax.broadcasted_iota(jnp.int32, logits.shape, 1)
    m1 = jnp.max(logits, axis=-1, keepdims=True)
    i1 = jnp.min(jnp.where(logits == m1, lane, ne), axis=-1, keepdims=True)
    rest = jnp.where(lane == i1, -jnp.inf, logits)
    m2 = jnp.max(rest, axis=-1, keepdims=True)
    i2 = jnp.min(jnp.where(rest == m2, lane, ne), axis=-1, keepdims=True)
    e2 = jnp.exp(m2 - m1)
    den = 1.0 + e2
    return jnp.where(lane == i1, 1.0 / den, 0.0) + jnp.where(lane == i2, e2 / den, 0.0)


def _moe_dense_kernel(x_ref, g_ref, wr_ref, wg_ref, wu_ref, wd_ref, gf_ref, o_ref, xn_ref, gate_ref, acc_ref):
    e = pl.program_id(1)
    f = pl.program_id(2)

    @pl.when((e == 0) & (f == 0))
    def _():
        xn = _rms(x_ref[...], g_ref[...])
        xn_ref[...] = xn.astype(xn_ref.dtype)
        logits = jnp.dot(xn, wr_ref[...], precision=lax.Precision.HIGHEST, preferred_element_type=F32)
        gate_ref[...] = _router_gates(logits)
        acc_ref[...] = jnp.zeros(acc_ref.shape, F32)

    xn = xn_ref[...]
    a = _silu(_mm(xn, wg_ref[...])) * _mm(xn, wu_ref[...])
    lane = lax.broadcasted_iota(jnp.int32, gate_ref.shape, 1)
    gate_e = jnp.sum(jnp.where(lane == e, gate_ref[...], 0.0), axis=-1, keepdims=True)
    acc_ref[...] += _mm(a, wd_ref[...]) * gate_e

    @pl.when((e == pl.num_programs(1) - 1) & (f == pl.num_programs(2) - 1))
    def _():
        o_ref[...] = _rms(x_ref[...] + acc_ref[...], gf_ref[...])


def _moe_final(x, g, w_router, wg, wu, wd, g_final, tm, tf):
    n, d = x.shape
    ne, _, ff = wg.shape
    tm = min(tm, n)
    assert n % tm == 0 and ff % tf == 0
    return pl.pallas_call(
        _moe_dense_kernel,
        grid=(n // tm, ne, ff // tf),
        in_specs=[pl.BlockSpec((tm, d), lambda i, e, f: (i, 0)),
                  pl.BlockSpec((1, d), lambda i, e, f: (0, 0)),
                  pl.BlockSpec((d, ne), lambda i, e, f: (0, 0)),
                  pl.BlockSpec((None, d, tf), lambda i, e, f: (e, 0, f)),
                  pl.BlockSpec((None, d, tf), lambda i, e, f: (e, 0, f)),
                  pl.BlockSpec((None, tf, d), lambda i, e, f: (e, f, 0)),
                  pl.BlockSpec((1, d), lambda i, e, f: (0, 0))],
        out_specs=pl.BlockSpec((tm, d), lambda i, e, f: (i, 0)),
        out_shape=jax.ShapeDtypeStruct((n, d), F32),
        scratch_shapes=[pltpu.VMEM((tm, d), MXU_DTYPE), pltpu.VMEM((tm, ne), F32), pltpu.VMEM((tm, d), F32)],
        compiler_params=_params("parallel", "arbitrary", "arbitrary"),
        name="moe_final",
    )(x, g.reshape(1, d), w_router, wg.astype(MXU_DTYPE), wu.astype(MXU_DTYPE), wd.astype(MXU_DTYPE),
      g_final.reshape(1, d))


MOE_TM = 512
META_ROWS = 8


def _moe_route_kernel(x_ref, g_ref, wrt_ref, xn_ref, mi_ref, mw_ref, cnt_ref, carry_ref):
    i = pl.program_id(0)
    tm = x_ref.shape[0]
    ne = wrt_ref.shape[0]

    @pl.when(i == 0)
    def _():
        carry_ref[...] = jnp.zeros(carry_ref.shape, F32)

    xn = _rms(x_ref[...], g_ref[...])
    xn_ref[...] = xn
    logits = lax.dot_general(wrt_ref[...], xn, (((1,), (1,)), ((), ())), precision=lax.Precision.HIGHEST,
                             preferred_element_type=F32)
    sub = _iota2(logits.shape, 0)
    m1 = jnp.max(logits, axis=0, keepdims=True)
    i1 = jnp.min(jnp.where(logits == m1, sub, ne), axis=0, keepdims=True)
    rest = jnp.where(sub == i1, -jnp.inf, logits)
    m2 = jnp.max(rest, axis=0, keepdims=True)
    i2 = jnp.min(jnp.where(rest == m2, sub, ne), axis=0, keepdims=True)
    e2 = jnp.exp(m2 - m1)
    den = 1.0 + e2
    onehot = jnp.where(sub == i1, 1.0, jnp.where(sub == i2, 1.0, 0.0))
    before = jnp.where(_iota2((tm, tm), 0) < _iota2((tm, tm), 1), 1.0, 0.0)
    rank = jnp.dot(onehot.astype(BF16), before.astype(BF16), preferred_element_type=F32) + carry_ref[:, 0:1]
    r1 = jnp.sum(jnp.where(sub == i1, rank, 0.0), axis=0, keepdims=True)
    r2 = jnp.sum(jnp.where(sub == i2, rank, 0.0), axis=0, keepdims=True)
    zi = jnp.zeros((META_ROWS - 4, tm), jnp.int32)
    mi_ref[...] = jnp.concatenate([i1, i2, r1.astype(jnp.int32), r2.astype(jnp.int32), zi], axis=0)
    mw_ref[...] = jnp.concatenate([1.0 / den, e2 / den, jnp.zeros((META_ROWS - 2, tm), F32)], axis=0)
    carry_ref[...] = carry_ref[...] + jnp.sum(onehot, axis=1, keepdims=True)
    cnt_ref[...] = carry_ref[...]


def _moe_route(x, g, w_router):
    n, d = x.shape
    ne = w_router.shape[1]
    tm = MOE_TM
    assert n % tm == 0 and ne == META_ROWS
    meta = pl.BlockSpec((None, META_ROWS, tm), lambda i: (i, 0, 0))
    return pl.pallas_call(
        _moe_route_kernel,
        grid=(n // tm,),
        in_specs=[pl.BlockSpec((tm, d), lambda i: (i, 0)),
                  pl.BlockSpec((1, d), lambda i: (0, 0)),
                  pl.BlockSpec((ne, d), lambda i: (0, 0))],
        out_specs=[pl.BlockSpec((tm, d), lambda i: (i, 0)), meta, meta,
                   pl.BlockSpec((ne, 128), lambda i: (0, 0))],
        out_shape=[jax.ShapeDtypeStruct((n, d), F32),
                   jax.ShapeDtypeStruct((n // tm, META_ROWS, tm), jnp.int32),
                   jax.ShapeDtypeStruct((n // tm, META_ROWS, tm), F32),
                   jax.ShapeDtypeStruct((ne, 128), F32)],
        scratch_shapes=[pltpu.VMEM((ne, 128), F32)],
        compiler_params=_params("arbitrary"),
        name="moe_route",
    )(x, g.reshape(1, d), w_router.T)


def _row_copy(src, src_row, dst, dst_row, sem):
    return pltpu.make_async_copy(src.at[pl.ds(src_row, 1), :], dst.at[pl.ds(dst_row, 1), :], sem)


def _moe_slots_kernel(off_ref, mi_ref, sl_ref):
    rows = []
    for slot in range(2):
        e = mi_ref[slot:slot + 1, :]
        base = jnp.zeros(e.shape, jnp.int32)
        for x in range(META_ROWS):
            base = jnp.where(e == x, off_ref[x], base)
        rows.append(base + mi_ref[2 + slot:3 + slot, :])
    sl_ref[...] = jnp.concatenate(rows + [jnp.zeros((META_ROWS - 2, mi_ref.shape[1]), jnp.int32)], axis=0)


def _moe_slots(offsets, meta_i):
    nt, _, tm = meta_i.shape
    meta = pl.BlockSpec((None, META_ROWS, tm), lambda i, off: (i, 0, 0))
    return pl.pallas_call(
        _moe_slots_kernel,
        grid_spec=pltpu.PrefetchScalarGridSpec(num_scalar_prefetch=1, grid=(nt,), in_specs=[meta], out_specs=meta),
        out_shape=jax.ShapeDtypeStruct(meta_i.shape, jnp.int32),
        compiler_params=_params("parallel"),
        name="moe_slots",
    )(offsets, meta_i)


def _moe_dispatch_kernel(sl_ref, x_ref, xs_in_ref, xs_ref, sem):
    del xs_in_ref
    tm = x_ref.shape[0]

    def start(r, carry):
        for slot in range(2):
            _row_copy(x_ref, r, xs_ref, sl_ref[slot, r], sem).start()
        return carry

    lax.fori_loop(0, tm, start, 0, unroll=8)
    for slot in range(2):
        pltpu.make_async_copy(x_ref, xs_ref.at[pl.ds(0, tm), :], sem).wait()


def _moe_dispatch(slots, xn, n_sorted):
    n, d = xn.shape
    tm = MOE_TM
    return pl.pallas_call(
        _moe_dispatch_kernel,
        grid=(n // tm,),
        in_specs=[pl.BlockSpec((None, META_ROWS, tm), lambda i: (i, 0, 0), memory_space=pltpu.SMEM),
                  pl.BlockSpec((tm, d), lambda i: (i, 0)),
                  pl.BlockSpec(memory_space=pl.ANY)],
        out_specs=pl.BlockSpec(memory_space=pl.ANY),
        scratch_shapes=[pltpu.SemaphoreType.DMA(())],
        out_shape=jax.ShapeDtypeStruct((n_sorted, d), F32),
        input_output_aliases={2: 0},
        compiler_params=_params("arbitrary"),
        name="moe_dispatch",
    )(slots, xn, jnp.zeros((n_sorted, d), F32))


def _moe_experts_kernel(te_ref, nu_ref, x_ref, wg_ref, wu_ref, wd_ref, o_ref, xb_ref, acc_ref):
    t = pl.program_id(0)
    f = pl.program_id(1)

    @pl.when(t < nu_ref[0])
    def _():
        @pl.when(f == 0)
        def _():
            xb_ref[...] = x_ref[...].astype(xb_ref.dtype)
            acc_ref[...] = jnp.zeros(acc_ref.shape, F32)

        xb = xb_ref[...]
        a = _silu(_mm(xb, wg_ref[...])) * _mm(xb, wu_ref[...])
        acc_ref[...] += _mm(a, wd_ref[...])

        @pl.when(f == pl.num_programs(1) - 1)
        def _():
            o_ref[...] = acc_ref[...]

    @pl.when((t >= nu_ref[0]) & (f == pl.num_programs(1) - 1))
    def _():
        o_ref[...] = jnp.zeros(o_ref.shape, F32)


def _moe_experts(tile_expert, n_used, xs, wg, wu, wd, tf):
    r, d = xs.shape
    ne, _, ff = wg.shape
    tm = MOE_TM
    nf = ff // tf
    assert r % tm == 0 and ff % tf == 0

    def row_tile(t, f, te, nu):
        return (jnp.minimum(t, nu[0] - 1), 0)

    def fcol(t, f, nu):
        return jnp.where(t < nu[0], f, nf - 1)

    grid_spec = pltpu.PrefetchScalarGridSpec(
        num_scalar_prefetch=2,
        grid=(r // tm, nf),
        in_specs=[pl.BlockSpec((tm, d), row_tile),
                  pl.BlockSpec((None, d, tf), lambda t, f, te, nu: (te[t], 0, fcol(t, f, nu))),
                  pl.BlockSpec((None, d, tf), lambda t, f, te, nu: (te[t], 0, fcol(t, f, nu))),
                  pl.BlockSpec((None, tf, d), lambda t, f, te, nu: (te[t], fcol(t, f, nu), 0))],
        out_specs=pl.BlockSpec((tm, d), lambda t, f, te, nu: (t, 0)),
        scratch_shapes=[pltpu.VMEM((tm, d), MXU_DTYPE), pltpu.VMEM((tm, d), F32)])
    return pl.pallas_call(
        _moe_experts_kernel,
        grid_spec=grid_spec,
        out_shape=jax.ShapeDtypeStruct((r, d), F32),
        compiler_params=_params("arbitrary", "arbitrary"),
        name="moe_experts",
    )(tile_expert, n_used, xs, wg.astype(MXU_DTYPE), wu.astype(MXU_DTYPE), wd.astype(MXU_DTYPE))


def _moe_combine_kernel(sl_ref, mw_ref, x_ref, ys_ref, gf_ref, o_ref, buf_ref, sem):
    tm = x_ref.shape[0]

    def start(r, carry):
        for slot in range(2):
            _row_copy(ys_ref, sl_ref[slot, r], buf_ref.at[slot], r, sem).start()
        return carry

    lax.fori_loop(0, tm, start, 0, unroll=8)
    for slot in range(2):
        pltpu.make_async_copy(ys_ref.at[pl.ds(0, tm), :], buf_ref.at[slot], sem).wait()
    w1 = _to_col(mw_ref[0:1, :])
    w2 = _to_col(mw_ref[1:2, :])
    o_ref[...] = _rms(x_ref[...] + (buf_ref[0] * w1 + buf_ref[1] * w2), gf_ref[...])


def _moe_combine(slots, meta_w, x, ys, g_final):
    n, d = x.shape
    tm = MOE_TM
    meta = lambda space: pl.BlockSpec((None, META_ROWS, tm), lambda i: (i, 0, 0), memory_space=space)
    return pl.pallas_call(
        _moe_combine_kernel,
        grid=(n // tm,),
        in_specs=[meta(pltpu.SMEM), meta(pltpu.VMEM),
                  pl.BlockSpec((tm, d), lambda i: (i, 0)),
                  pl.BlockSpec(memory_space=pl.ANY),
                  pl.BlockSpec((1, d), lambda i: (0, 0))],
        out_specs=pl.BlockSpec((tm, d), lambda i: (i, 0)),
        scratch_shapes=[pltpu.VMEM((2, tm, d), F32), pltpu.SemaphoreType.DMA(())],
        out_shape=jax.ShapeDtypeStruct((n, d), F32),
        compiler_params=_params("arbitrary"),
        name="moe_combine",
    )(slots, meta_w, x, ys, g_final.reshape(1, d))


def _moe_routed_final(x, g, w_router, wg, wu, wd, g_final, tf):
    n, d = x.shape
    ne = w_router.shape[1]
    tm = MOE_TM
    xn, meta_i, meta_w, counts = _moe_route(x, g, w_router)
    counts = counts[:, 0].astype(jnp.int32)
    padded = (counts + tm - 1) // tm * tm
    ends = jnp.cumsum(padded)
    offsets = ends - padded
    n_tiles = (2 * n) // tm + ne
    n_used = (ends[-1] // tm).reshape(1)
    tile_start = jnp.arange(n_tiles, dtype=jnp.int32) * tm
    tile_expert = jnp.sum(tile_start[:, None] >= ends[None, :], axis=1).astype(jnp.int32)
    tile_expert = jnp.minimum(tile_expert, tile_expert[jnp.maximum(n_used[0] - 1, 0)])
    slots = _moe_slots(offsets, meta_i)
    xs = _moe_dispatch(slots, xn, n_tiles * tm)
    ys = _moe_experts(tile_expert, n_used, xs, wg, wu, wd, tf)
    return _moe_combine(slots, meta_w, x, ys, g_final)


ODD_RWKV_COLS = 3 * D_W + 256
ODD_SPLITS = (C_K, C_K, C_V, C_V, D_W, D_W, D_W, 256)


def _odd_weight_layout(w_in, mu):
    o = HGRN_IN
    r = slice(o, o + D_W)
    wd = slice(o + D_W, o + D_W + D_DECAY_LORA)
    k = slice(wd.stop, wd.stop + D_W)
    v = slice(k.stop, k.stop + D_W)
    rest = slice(v.stop, v.stop + D_AAA_LORA + D_GATE_LORA)
    pad = 256 - (D_DECAY_LORA + D_AAA_LORA + D_GATE_LORA)
    w = jnp.concatenate([w_in[:, :o], w_in[:, r], w_in[:, k], w_in[:, v], w_in[:, wd], w_in[:, rest],
                         jnp.zeros((w_in.shape[0], pad), w_in.dtype)], axis=1)
    ro = lambda s: slice(s.start - o, s.stop - o)
    m = jnp.concatenate([mu[ro(r)], mu[ro(k)], mu[ro(v)], mu[ro(wd)], mu[ro(rest)], jnp.zeros((pad,), mu.dtype)])
    return w, m


def _odd_in_proj_kernel(x_ref, g_ref, w_ref, mu_ref, *refs, shift, tiles_per_seq):
    if shift:
        (cq, cf, ci, cg, r, k, v, lo, xs_ref, carry_ref) = refs
        xn = _rms(x_ref[...], g_ref[...])
        xs_ref[...] = xn[xn.shape[0] - 1:, :]
    else:
        (xp_ref, cq, cf, ci, cg, r, k, v, lo, xs_ref) = refs
        xn = _rms(x_ref[...], g_ref[...])
        xs_ref[...] = xn
    y = xn.astype(MXU_DTYPE)
    off = 0
    for o_ref in (cq, cf, ci, cg):
        o_ref[...] = jnp.dot(y, w_ref[:, off:off + C_K], preferred_element_type=F32)
        off += C_K
    z = jnp.dot(y, w_ref[:, off:], preferred_element_type=F32)
    if shift:
        i = pl.program_id(0)

        @pl.when(i % tiles_per_seq == 0)
        def _():
            carry_ref[...] = jnp.zeros(carry_ref.shape, F32)

        row = lax.broadcasted_iota(jnp.int32, z.shape, 0)
        zp = jnp.where(row == 0, carry_ref[...], pltpu.roll(z, 1, 0))
        carry_ref[...] = z[z.shape[0] - 1:, :]
    else:
        zp = jnp.dot(xp_ref[...].astype(MXU_DTYPE), w_ref[:, off:], preferred_element_type=F32)
    z = z + (zp - z) * mu_ref[...]
    r[...] = z[:, :D_W]
    k[...] = z[:, D_W:2 * D_W]
    v[...] = z[:, 2 * D_W:3 * D_W]
    lo[...] = z[:, 3 * D_W:]


def _odd_in_proj(x, g, w, mu, tm, seq_len=None, x_prev=None):
    n, d = x.shape
    tm = min(tm, n)
    shift = x_prev is None
    assert n % tm == 0
    if shift:
        assert seq_len % tm == 0
    tiles_per_seq = seq_len // tm if shift else 1
    outs = [jax.ShapeDtypeStruct((n, s), F32) for s in ODD_SPLITS]
    out_specs = [pl.BlockSpec((tm, s), lambda i: (i, 0)) for s in ODD_SPLITS]
    in_specs = [pl.BlockSpec((tm, d), lambda i: (i, 0)),
                pl.BlockSpec((1, d), lambda i: (0, 0)),
                pl.BlockSpec(w.shape, lambda i: (0, 0)),
                pl.BlockSpec((1, ODD_RWKV_COLS), lambda i: (0, 0))]
    args = [x, g.reshape(1, d), w.astype(MXU_DTYPE), mu.reshape(1, ODD_RWKV_COLS)]
    scratch = []
    if shift:
        nseq = n // seq_len
        outs.append(jax.ShapeDtypeStruct((nseq, 1, d), F32))
        out_specs.append(pl.BlockSpec((None, 1, d), lambda i: (i // tiles_per_seq, 0, 0)))
        scratch.append(pltpu.VMEM((1, ODD_RWKV_COLS), F32))
    else:
        in_specs.append(pl.BlockSpec((tm, d), lambda i: (i, 0)))
        args.append(x_prev)
        outs.append(jax.ShapeDtypeStruct((n, d), F32))
        out_specs.append(pl.BlockSpec((tm, d), lambda i: (i, 0)))
    return pl.pallas_call(
        functools.partial(_odd_in_proj_kernel, shift=shift, tiles_per_seq=tiles_per_seq),
        grid=(n // tm,),
        in_specs=in_specs,
        out_specs=out_specs,
        out_shape=outs,
        scratch_shapes=scratch,
        compiler_params=_params("arbitrary"),
        name="odd_in_proj",
    )(*args)


def _split3(x):
    if MXU_DTYPE == F32:
        return (x,)
    x1 = x.astype(MXU_DTYPE)
    r1 = x - x1.astype(F32)
    x2 = r1.astype(MXU_DTYPE)
    x3 = (r1 - x2.astype(F32)).astype(MXU_DTYPE)
    return (x1, x2, x3)


def _mm_exact_lhs(ones, x):
    o = ones.astype(MXU_DTYPE)
    return sum(jnp.dot(o, p, preferred_element_type=F32) for p in _split3(x))


def _mm_exact_rhs(x, ones):
    o = ones.astype(MXU_DTYPE)
    return sum(jnp.dot(p, o, preferred_element_type=F32) for p in _split3(x))


def _iota2(shape, dim):
    return lax.broadcasted_iota(jnp.int32, shape, dim)


def _group_ones(n, width):
    return jnp.where(_iota2((n, n), 0) // width == _iota2((n, n), 1) // width, 1.0, 0.0)


HGRN_CHUNK = 32
REC_TILE = 256


def _hgrn_forget(cf, lb_ref):
    l0 = lb_ref[0:1, :]
    l1 = lb_ref[1:2, :]
    mx = jnp.maximum(l0, l1)
    e0 = jnp.exp(l0 - mx)
    e1 = jnp.exp(l1 - mx)
    w0 = e0 / (e0 + e1)
    w1 = e1 / (e0 + e1)
    lb = (w0 + w1) - w0
    return lb + (1.0 - lb) * jax.nn.sigmoid(cf)


def _hgrn_kernel(cq_ref, cf_ref, ci_ref, cg_ref, lb_ref, g_ref, o_ref, s_out_ref, s_ref):
    si = pl.program_id(1)
    t = REC_TILE
    c = HGRN_CHUNK
    nc = t // c

    @pl.when(si == 0)
    def _():
        s_ref[...] = jnp.zeros(s_ref.shape, F32)

    f = _hgrn_forget(cf_ref[...], lb_ref)
    q = _silu(cq_ref[...])
    k = 1.0 - f
    logf = jnp.log(f)
    inp = ci_ref[...]
    row = _iota2((t, t), 0)
    col = _iota2((t, t), 1)
    same = row // c == col // c
    cum_m = jnp.where(same & (col <= row), 1.0, 0.0)
    ref_m = jnp.where(same & (col % c <= c // 2), 1.0, 0.0)
    last_m = jnp.where(same, 1.0, 0.0)
    b = _mm_exact_lhs(cum_m, logf)
    b_ref = _mm_exact_lhs(ref_m, logf)
    b_last = _mm_exact_lhs(last_m, logf)
    sel8 = jnp.where(_iota2((nc, t), 0) == _iota2((nc, t), 1) // c, 1.0, 0.0)
    dec_t = jnp.exp(_mm_exact_lhs(sel8, logf)).T
    qs = (q * jnp.exp(b - b_ref)).astype(MXU_DTYPE)
    ks = (k * jnp.exp(b_ref - b)).astype(MXU_DTYPE)
    q_inter = (q * jnp.exp(b)).astype(MXU_DTYPE)
    k_state = (k * jnp.exp(b_last - b)).astype(MXU_DTYPE)
    inp_m = inp.astype(MXU_DTYPE)
    causal = same & (col <= row)
    outs = []
    for h in range(HC):
        cs = slice(h * DKC, (h + 1) * DKC)
        vs = slice(h * DVC, (h + 1) * DVC)
        scores = jnp.where(causal, _mm_nt(qs[:, cs], ks[:, cs]), 0.0)
        o_h = _mm(scores, inp_m[:, vs])
        state = s_ref[h]
        inter = []
        for j in range(nc):
            rs = slice(j * c, (j + 1) * c)
            inter.append(_mm(q_inter[rs, cs], state))
            upd = lax.dot_general(k_state[rs, cs], inp_m[rs, vs], (((0,), (0,)), ((), ())),
                                  preferred_element_type=F32)
            state = state * dec_t[cs, j:j + 1] + upd
        s_ref[h] = state
        o_h = o_h + jnp.concatenate(inter, axis=0)
        outs.append(_rms(o_h, g_ref[...]))
    o = jnp.concatenate(outs, axis=1) * _silu(cg_ref[...])
    o_ref[...] = o.astype(o_ref.dtype)

    @pl.when(si == pl.num_programs(1) - 1)
    def _():
        s_out_ref[...] = s_ref[...]


def _hgrn_prompt(cq, cf, ci, cg, hgrn_lb, norm_g):
    b, tt, _ = cq.shape
    t = REC_TILE
    assert tt % t == 0
    blk = pl.BlockSpec((None, t, C_K), lambda bi, si: (bi, si, 0))
    return pl.pallas_call(
        _hgrn_kernel,
        grid=(b, tt // t),
        in_specs=[blk, blk, blk, blk,
                  pl.BlockSpec(hgrn_lb.shape, lambda bi, si: (0, 0)),
                  pl.BlockSpec((1, DVC), lambda bi, si: (0, 0))],
        out_specs=[pl.BlockSpec((None, t, C_V), lambda bi, si: (bi, si, 0)),
                   pl.BlockSpec((None, HC, DKC, DVC), lambda bi, si: (bi, 0, 0, 0))],
        out_shape=[jax.ShapeDtypeStruct((b, tt, C_V), BF16), jax.ShapeDtypeStruct((b, HC, DKC, DVC), F32)],
        scratch_shapes=[pltpu.VMEM((HC, DKC, DVC), F32)],
        compiler_params=_params("parallel", "arbitrary"),
        name="hgrn_prompt",
    )(cq, cf, ci, cg, hgrn_lb, norm_g.reshape(1, DVC))


RWKV_CHUNK = 32
LORA_COLS = 256


def _rwkv_param_rows(w0, a0, k_k, k_a, r_k, ln_w, ln_b):
    return jnp.stack([w0, a0, k_k, k_a, r_k.reshape(-1), ln_w, ln_b, jnp.zeros_like(w0)]).astype(F32)


def _rwkv_lora_weights(w2, a2, g2):
    z = jnp.zeros((3, LORA_COLS, D_W), F32)
    z = z.at[0, :D_DECAY_LORA].set(w2)
    z = z.at[1, D_DECAY_LORA:D_DECAY_LORA + D_AAA_LORA].set(a2)
    o = D_DECAY_LORA + D_AAA_LORA
    return z.at[2, o:o + D_GATE_LORA].set(g2)


def _dot_hi(a, b):
    return jnp.dot(a, b, precision=lax.Precision.HIGHEST, preferred_element_type=F32)


def _rwkv_activations(r, k, v, lo, par_ref, lw_ref):
    w0, a0, k_k, k_a, r_k = (par_ref[i:i + 1, :] for i in range(5))
    x = -(w0 + _dot_hi(jnp.tanh(lo), lw_ref[0]))
    softplus = jnp.maximum(x, 0.0) + jnp.log(1.0 + jnp.exp(-jnp.abs(x)))
    wlog = -jnp.exp(-softplus - 0.5)
    a = jax.nn.sigmoid(a0 + _dot_hi(lo, lw_ref[1]))
    g = _dot_hi(jax.nn.sigmoid(lo), lw_ref[2])
    ones = _group_ones(D_W, DD)
    kk = k * k_k
    norm = jnp.sqrt(_mm_exact_rhs(kk * kk, ones))
    kk = kk / jnp.maximum(norm, 1e-12)
    k2 = k * (1.0 + (a - 1.0) * k_a)
    bonus = _mm_exact_rhs(r * k2 * r_k, ones) * v
    return wlog, k2, kk, kk * a, g, bonus


def _rwkv_finish(y, bonus, g, par_ref):
    ln_w = par_ref[5:6, :]
    ln_b = par_ref[6:7, :]
    ones = _group_ones(D_W, DD)
    mu = _mm_exact_rhs(y, ones) * (1.0 / DD)
    d = y - mu
    var = _mm_exact_rhs(d * d, ones) * (1.0 / DD)
    return (d * lax.rsqrt(var + RWKV_LN_EPS) * ln_w + ln_b + bonus) * g


def _rwkv_prep_kernel(r_ref, k_ref, v_ref, lo_ref, par_ref, lw_ref, wl_ref, k2_ref, kk_ref, kka_ref, g_ref, bo_ref):
    outs = _rwkv_activations(r_ref[...], k_ref[...], v_ref[...], lo_ref[...], par_ref, lw_ref)
    for o_ref, val in zip((wl_ref, k2_ref, kk_ref, kka_ref, g_ref, bo_ref), outs):
        o_ref[...] = val


def _rwkv_prep(r, k, v, lo, par, lw, tm):
    n = r.shape[0]
    tm = min(tm, n)
    assert n % tm == 0
    blk = pl.BlockSpec((tm, D_W), lambda i: (i, 0))
    return pl.pallas_call(
        _rwkv_prep_kernel,
        grid=(n // tm,),
        in_specs=[blk, blk, blk, pl.BlockSpec((tm, LORA_COLS), lambda i: (i, 0)),
                  pl.BlockSpec(par.shape, lambda i: (0, 0)),
                  pl.BlockSpec(lw.shape, lambda i: (0, 0, 0))],
        out_specs=[blk] * 6,
        out_shape=[jax.ShapeDtypeStruct((n, D_W), F32)] * 6,
        compiler_params=_params("parallel"),
        name="rwkv_prep",
    )(r, k, v, lo, par, lw)


def _pack_heads(x):
    return jnp.concatenate([x[:, h * DD:(h + 1) * DD] for h in range(HD)], axis=0)


def _unpack_heads(x):
    c = x.shape[0] // HD
    return jnp.concatenate([x[h * c:(h + 1) * c, :] for h in range(HD)], axis=1)


def _rwkv_masks(c):
    n = HD * c
    row = _iota2((n, n), 0)
    col = _iota2((n, n), 1)
    same = row // c == col // c
    keep = _iota2((n, HD * DD), 0) // c == _iota2((n, HD * DD), 1) // DD
    tri = jnp.where(_iota2((c, c), 1) <= _iota2((c, c), 0), 1.0, 0.0)
    return same & (col < row), same & (col <= row), jnp.where(row == col, 1.0, 0.0), keep, tri


def _tile_heads(x, keep):
    return jnp.where(keep, jnp.concatenate([x] * HD, axis=0), 0.0).astype(MXU_DTYPE)


def _rwkv_chunk(r, wl, k, v, al, be, ht, masks):
    c = r.shape[0]
    strict, incl, eye, keep, tri = masks
    cum = _mm_exact_lhs(tri, wl)
    e_neg = jnp.exp(-cum)
    r_hat = r * jnp.exp(cum)
    a_hat = al * jnp.exp(cum - wl)
    k_til = k * e_neg
    b_til = be * e_neg
    pr, pa, pk, pb = (_pack_heads(x).astype(MXU_DTYPE) for x in (r_hat, a_hat, k_til, b_til))
    pv_t = _pack_heads(v).T.astype(MXU_DTYPE)
    nmat = jnp.where(strict, _mm_nt(pa, pb), 0.0)
    a_k = jnp.where(strict, _mm_nt(pa, pk), 0.0)
    r_k = jnp.where(incl, _mm_nt(pr, pk), 0.0)
    r_b = jnp.where(incl, _mm_nt(pr, pb), 0.0)
    inv = eye - nmat
    pw = nmat
    steps = int(math.log2(c))
    assert 2 ** steps == c
    for _ in range(steps - 1):
        pw = _mm(pw, pw)
        inv = inv + _mm(inv, pw)
    w_t = _mm_nt(ht, _tile_heads(a_hat, keep)) + _mm_nt(pv_t, a_k)
    u_t = _mm_nt(w_t, inv)
    y_t = _mm_nt(ht, _tile_heads(r_hat, keep)) + _mm_nt(pv_t, r_k) - _mm_nt(u_t, r_b)
    gam = jnp.exp(cum[c - 1:c, :])
    ht_new = gam * (ht + _mm(pv_t, _tile_heads(k_til, keep)) - _mm(u_t, _tile_heads(b_til, keep)))
    return _unpack_heads(y_t.T), ht_new


RWKV_SEQS = 2


def _rwkv_kernel(r_ref, wl_ref, k_ref, v_ref, al_ref, be_ref, g_ref, bo_ref, par_ref, o_ref, s_out_ref, h_ref, y_ref):
    si = pl.program_id(1)
    c = RWKV_CHUNK
    nseq = r_ref.shape[0]

    @pl.when(si == 0)
    def _():
        h_ref[...] = jnp.zeros(h_ref.shape, F32)

    masks = _rwkv_masks(c)

    states = [h_ref[s] for s in range(nseq)]
    for j in range(REC_TILE // c):
        rows = slice(j * c, (j + 1) * c)
        for s in range(nseq):
            y, states[s] = _rwkv_chunk(r_ref[s, rows, :], wl_ref[s, rows, :], k_ref[s, rows, :], v_ref[s, rows, :],
                                       al_ref[s, rows, :], be_ref[s, rows, :], states[s], masks)
            y_ref[s, rows, :] = y
    for s in range(nseq):
        h_ref[s] = states[s]
    for s in range(nseq):
        o_ref[s] = _rwkv_finish(y_ref[s], bo_ref[s], g_ref[s], par_ref).astype(o_ref.dtype)

    @pl.when(si == pl.num_programs(1) - 1)
    def _():
        s_out_ref[...] = h_ref[...]


def _rwkv_prompt(r, wl, k, v, al, be, g, bonus, par):
    b, tt, _ = r.shape
    t = REC_TILE
    ns = RWKV_SEQS if b % RWKV_SEQS == 0 else 1
    assert tt % t == 0
    blk = pl.BlockSpec((ns, t, D_W), lambda bi, si: (bi, si, 0))
    return pl.pallas_call(
        _rwkv_kernel,
        grid=(b // ns, tt // t),
        in_specs=[blk] * 8 + [pl.BlockSpec(par.shape, lambda bi, si: (0, 0))],
        out_specs=[blk, pl.BlockSpec((ns, DD, HD * DD), lambda bi, si: (bi, 0, 0))],
        out_shape=[jax.ShapeDtypeStruct((b, tt, D_W), BF16), jax.ShapeDtypeStruct((b, DD, HD * DD), F32)],
        scratch_shapes=[pltpu.VMEM((ns, DD, HD * DD), F32), pltpu.VMEM((ns, t, D_W), F32)],
        compiler_params=_params("parallel", "arbitrary"),
        name="rwkv_prompt",
    )(r, wl, k, v, al, be, g, bonus, par)


def _head_rows(bl_ref, first, count, reps):
    rows = []
    for i in range(count):
        rows += [bl_ref[first + i, 0, 0:1, :]] * reps
    return jnp.concatenate(rows, axis=0) if len(rows) > 1 else rows[0]


DECODE_PAGES = 8


def _head_rows_of_page(page_ref, h, n_heads):
    return page_ref[pl.ds(h, PAGE_SIZE, stride=n_heads), :]


def _decode_diff_kernel(pt_ref, tab_ref, lv_ref, q_ref, kn_ref, vn_ref, *refs):
    pp = DECODE_PAGES
    ck, cv, cm = refs[:pp], refs[pp:2 * pp], refs[2 * pp:3 * pp]
    bl_ref, g_ref, o_ref, sums_ref, m_ref, l_ref, acc_ref, qm_ref = refs[3 * pp:]
    p = pl.program_id(1)
    last = pl.num_programs(1) - 1
    nr = 2 * HA
    rowc = _iota2((nr, 1), 0)

    def head_col(bucket):
        col = jnp.zeros((nr, 1), F32)
        for h in range(HA):
            col = jnp.where(rowc // 2 == h, tab_ref[bucket, h], col)
        return col

    @pl.when(p == 0)
    def _():
        q = q_ref[...] * (DA ** -0.5)
        keep = _iota2((nr, A_QK), 1) // DA == _iota2((nr, A_QK), 0)
        qm = jnp.where(keep, jnp.broadcast_to(q, (nr, A_QK)), 0.0)
        qm_ref[...] = qm
        m_ref[...] = jnp.sum(qm * kn_ref[...], axis=-1, keepdims=True) + head_col(0)
        l_ref[...] = jnp.ones(l_ref.shape, F32)
        for h in range(HA):
            acc_ref[h] = jnp.broadcast_to(vn_ref[:, h * DVA:(h + 1) * DVA], (nr, DVA))

    qm = qm_ref[...].astype(MXU_DTYPE)
    s = jnp.concatenate([jnp.dot(qm, ck[j][...].reshape(A_QK, PAGE_SIZE).astype(MXU_DTYPE),
                                 preferred_element_type=F32) for j in range(pp)], axis=1)
    far = head_col(N_BUCKETS - 1)
    near = jnp.concatenate([jnp.broadcast_to(far, (nr, (pp - 1) * PAGE_SIZE)), _head_rows(bl_ref, 0, HA, 2)], axis=1)
    s = s + jnp.where(p == last, near, far)
    m_prev = m_ref[...]
    m_new = jnp.maximum(m_prev, jnp.max(s, axis=-1, keepdims=True))
    alpha = jnp.exp(m_prev - m_new)
    pe = jnp.exp(s - m_new)
    l_ref[...] = alpha * l_ref[...] + jnp.sum(pe, axis=-1, keepdims=True)
    for h in range(HA):
        vh = jnp.concatenate([_head_rows_of_page(cv[j], h, HA) for j in range(pp)], axis=0)
        acc_ref[h] = alpha * acc_ref[h] + _mm(pe, vh)
    m_ref[...] = m_new
    for j in range(pp):
        part = jnp.sum(cm[j][...].reshape(PAGE_SIZE * HB // 8, 8, DB), axis=0)
        sums_ref[j] = part[0:HB] + part[HB:2 * HB]

    @pl.when(p == last)
    def _():
        lam = _diff_lambda(lv_ref)
        pieces = []
        for h in range(HA):
            outs = acc_ref[h] / l_ref[...]
            o_h = outs[2 * h:2 * h + 1, :] - lam * outs[2 * h + 1:2 * h + 2, :]
            pieces.append(_rms(o_h, g_ref[...]) * (1.0 - LAM_INIT_0))
        o_ref[...] = jnp.concatenate(pieces, axis=1)


def _decode_diff(page_table, rel_bias, lam_vecs, q, k_new, v_new, ckd_t, cvd, ckm, bias_last, subln_g):
    db, n_pages = page_table.shape
    pp = DECODE_PAGES
    assert n_pages % pp == 0
    row = pl.BlockSpec((None, 1, A_V), lambda b, p, pt: (b, 0, 0))

    def page_spec(block, j):
        zeros = (0,) * len(block)
        return pl.BlockSpec((None,) + block, lambda b, p, pt: (pt[b * n_pages + p * pp + j],) + zeros)

    kt_pages = [page_spec((HA, 2, DA, PAGE_SIZE), j) for j in range(pp)]
    v_pages = [page_spec((PAGE_SIZE * HA, DVA), j) for j in range(pp)]
    m_pages = [page_spec((PAGE_SIZE * HB, DB), j) for j in range(pp)]
    grid_spec = pltpu.PrefetchScalarGridSpec(
        num_scalar_prefetch=1,
        grid=(db, n_pages // pp),
        in_specs=[pl.BlockSpec(memory_space=pltpu.SMEM),
                  pl.BlockSpec((4, DA), lambda b, p, pt: (0, 0)),
                  row, row, row, *kt_pages, *v_pages, *m_pages,
                  pl.BlockSpec(bias_last.shape, lambda b, p, pt: (0, 0, 0, 0)),
                  pl.BlockSpec((1, DVA), lambda b, p, pt: (0, 0))],
        out_specs=[row, pl.BlockSpec((None, pp, HB, DB), lambda b, p, pt: (b, p, 0, 0))],
        scratch_shapes=[pltpu.VMEM((2 * HA, 1), F32), pltpu.VMEM((2 * HA, 1), F32),
                        pltpu.VMEM((HA, 2 * HA, DVA), F32), pltpu.VMEM((2 * HA, A_QK), F32)])
    return pl.pallas_call(
        _decode_diff_kernel,
        grid_spec=grid_spec,
        out_shape=[jax.ShapeDtypeStruct((db, 1, A_V), F32), jax.ShapeDtypeStruct((db, n_pages, HB, DB), F32)],
        compiler_params=_params("parallel", "arbitrary"),
        name="decode_diff",
    )(page_table.reshape(-1), rel_bias, lam_vecs, q, k_new, v_new, *([ckd_t] * pp), *([cvd] * pp), *([ckm] * pp),
      bias_last, subln_g.reshape(1, DVA))


SEL_LANES = 128


def _decode_gate_kernel(sums_ref, q_ref, sel_ref):
    n_pages = sums_ref.shape[0]
    ppb = MOBA_BLOCK // PAGE_SIZE
    nb = n_pages // ppb
    pair = jnp.where(_iota2((nb, n_pages), 1) // ppb == _iota2((nb, n_pages), 0), 1.0, 0.0)
    blk = _iota2((nb, 1), 0)
    lane = _iota2((1, SEL_LANES), 1)
    out = jnp.zeros((1, SEL_LANES), jnp.int32)
    for h in range(HB):
        means = _mm_exact_lhs(pair, sums_ref[:, h, :]) * (1.0 / MOBA_BLOCK)
        gate = jnp.sum(means * q_ref[:, h * DB:(h + 1) * DB], axis=-1, keepdims=True)
        for slot in range(MOBA_TOPK):
            best = jnp.max(gate, axis=0, keepdims=True)
            idx = jnp.min(jnp.where(gate == best, blk, nb), axis=0, keepdims=True)
            out = jnp.where(lane == h * MOBA_TOPK + slot, idx, out)
            gate = jnp.where(blk == idx, -jnp.inf, gate)
    sel_ref[...] = out


def _decode_gate(sums, q):
    db, n_pages = sums.shape[:2]
    assert n_pages % (MOBA_BLOCK // PAGE_SIZE) == 0 and n_pages * PAGE_SIZE // MOBA_BLOCK >= MOBA_TOPK
    return pl.pallas_call(
        _decode_gate_kernel,
        grid=(db,),
        in_specs=[pl.BlockSpec((None, n_pages, HB, DB), lambda b: (b, 0, 0, 0)),
                  pl.BlockSpec((None, 1, B_W), lambda b: (b, 0, 0))],
        out_specs=pl.BlockSpec((None, 1, SEL_LANES), lambda b: (b, 0, 0)),
        out_shape=jax.ShapeDtypeStruct((db, 1, SEL_LANES), jnp.int32),
        compiler_params=_params("parallel"),
        name="decode_gate",
    )(sums, q)


def _decode_moba_kernel(pt_ref, sel_ref, tab_ref, q_ref, kn_ref, vn_ref, *refs, n_pages):
    ppb = MOBA_BLOCK // PAGE_SIZE
    npg = MOBA_TOPK * ppb
    ck, cv = refs[:npg], refs[npg:2 * npg]
    bl_ref, o_ref = refs[2 * npg:]
    b = pl.program_id(0)
    h = pl.program_id(1)
    scale = DB ** -0.5
    q = q_ref[...]
    q8 = jnp.broadcast_to(q, (8, DB))
    s_new = jnp.sum(q * kn_ref[...], axis=-1, keepdims=True) * scale + tab_ref[0, HA + h]
    far = tab_ref[N_BUCKETS - 1, HA + h]

    def compute(hs):
        pieces = []
        for s in range(npg):
            page = sel_ref[b * SEL_LANES + h * MOBA_TOPK + s // ppb] * ppb + s % ppb
            sc = _mm_nt(q8, _head_rows_of_page(ck[s], hs, HB))[0:1, :] * scale
            pieces.append(sc + jnp.where(page == n_pages - 1, bl_ref[HA + hs, 0, 0:1, :], far))
        sc = jnp.concatenate(pieces, axis=1)
        m = jnp.maximum(jnp.max(sc, axis=-1, keepdims=True), s_new)
        e_new = jnp.exp(s_new - m)
        pe = jnp.exp(sc - m)
        acc = e_new * vn_ref[...]
        for s in range(npg):
            pe_s = jnp.broadcast_to(pe[:, s * PAGE_SIZE:(s + 1) * PAGE_SIZE], (8, PAGE_SIZE))
            acc = acc + _mm(pe_s, _head_rows_of_page(cv[s], hs, HB))[0:1, :]
        o_ref[...] = acc / (e_new + jnp.sum(pe, axis=-1, keepdims=True))

    for hs in range(HB):
        pl.when(h == hs)(functools.partial(compute, hs))


def _decode_moba(page_table, sel, rel_bias, q, k_new, v_new, ckm, cvm, bias_last):
    db, n_pages = page_table.shape
    ppb = MOBA_BLOCK // PAGE_SIZE
    npg = MOBA_TOPK * ppb
    row = pl.BlockSpec((None, 1, DB), lambda b, h, pt, sl: (b, 0, h))

    def page_spec(s):
        def index(b, h, pt, sl):
            blk = sl[b * SEL_LANES + h * MOBA_TOPK + s // ppb]
            return (pt[b * n_pages + blk * ppb + s % ppb], 0, 0)
        return pl.BlockSpec((None, PAGE_SIZE * HB, DB), index)

    pages = [page_spec(s) for s in range(npg)]
    grid_spec = pltpu.PrefetchScalarGridSpec(
        num_scalar_prefetch=2,
        grid=(db, HB),
        in_specs=[pl.BlockSpec(memory_space=pltpu.SMEM), row, row, row, *pages, *pages,
                  pl.BlockSpec(bias_last.shape, lambda b, h, pt, sl: (0, 0, 0, 0))],
        out_specs=row)
    return pl.pallas_call(
        functools.partial(_decode_moba_kernel, n_pages=n_pages),
        grid_spec=grid_spec,
        out_shape=jax.ShapeDtypeStruct((db, 1, B_W), F32),
        compiler_params=_params("parallel", "arbitrary"),
        name="decode_moba",
    )(page_table.reshape(-1), sel.reshape(-1), rel_bias, q, k_new, v_new, *([ckm] * npg), *([cvm] * npg), bias_last)


def _to_col(row):
    n = row.shape[1]
    eye = _iota2((n, n), 0) == _iota2((n, n), 1)
    return jnp.sum(jnp.where(eye, row, 0.0), axis=1, keepdims=True)


def _to_row(col):
    n = col.shape[0]
    eye = _iota2((n, n), 0) == _iota2((n, n), 1)
    return jnp.sum(jnp.where(eye, col, 0.0), axis=0, keepdims=True)


def _odd_step_kernel(cq_ref, cf_ref, ci_ref, cg_ref, r_ref, wl_ref, k_ref, v_ref, kk_ref, kka_ref, g_ref, bo_ref,
                     sh_ref, sr_ref, lb_ref, ng_ref, par_ref, o_ref, sh_out_ref, sr_out_ref):
    f = _hgrn_forget(cf_ref[...], lb_ref)
    q = _silu(cq_ref[...])
    k = 1.0 - f
    inp = ci_ref[...]
    outs = []
    for h in range(HC):
        cs = slice(h * DKC, (h + 1) * DKC)
        vs = slice(h * DVC, (h + 1) * DVC)
        s_new = sh_ref[h] * _to_col(f[:, cs]) + _to_col(k[:, cs]) * inp[:, vs]
        sh_out_ref[h] = s_new
        outs.append(_rms(jnp.sum(s_new * _to_col(q[:, cs]), axis=0, keepdims=True), ng_ref[...]))
    o_c = jnp.concatenate(outs, axis=1) * _silu(cg_ref[...])
    r = r_ref[...]
    w = jnp.exp(wl_ref[...])
    k2 = k_ref[...]
    v = v_ref[...]
    kk = kk_ref[...]
    kka = kka_ref[...]
    ys = []
    for h in range(HD):
        cs = slice(h * DD, (h + 1) * DD)
        s_old = sr_ref[h]
        sa = jnp.sum(s_old * (-kk[:, cs]), axis=1, keepdims=True)
        s_new = s_old * w[:, cs] + sa * kka[:, cs] + _to_col(v[:, cs]) * k2[:, cs]
        sr_out_ref[h] = s_new
        ys.append(_to_row(jnp.sum(s_new * r[:, cs], axis=1, keepdims=True)))
    y = jnp.broadcast_to(jnp.concatenate(ys, axis=1), (8, D_W))
    o_d = _rwkv_finish(y, bo_ref[...], g_ref[...], par_ref)[0:1, :]
    o_ref[...] = jnp.concatenate([o_c, o_d], axis=1)


def _odd_step(rows, s_hgrn, s_rwkv, hgrn_lb, norm_g, par):
    db = s_hgrn.shape[0]
    row = pl.BlockSpec((None, 1, D_W), lambda b: (b, 0, 0))
    sh = pl.BlockSpec((None, HC, DKC, DVC), lambda b: (b, 0, 0, 0))
    sr = pl.BlockSpec((None, HD, DD, DD), lambda b: (b, 0, 0, 0))
    return pl.pallas_call(
        _odd_step_kernel,
        grid=(db,),
        in_specs=[row] * 12 + [sh, sr,
                               pl.BlockSpec(hgrn_lb.shape, lambda b: (0, 0)),
                               pl.BlockSpec((1, DVC), lambda b: (0, 0)),
                               pl.BlockSpec(par.shape, lambda b: (0, 0))],
        out_specs=[pl.BlockSpec((None, 1, C_V + D_W), lambda b: (b, 0, 0)), sh, sr],
        out_shape=[jax.ShapeDtypeStruct((db, 1, C_V + D_W), F32),
                   jax.ShapeDtypeStruct(s_hgrn.shape, F32), jax.ShapeDtypeStruct(s_rwkv.shape, F32)],
        compiler_params=_params("parallel"),
        name="odd_step",
    )(*rows, s_hgrn, s_rwkv, hgrn_lb, norm_g.reshape(1, DVC), par)


TM = 512
FFN_TF = 1408
MOE_TF = 1792


def kernel(x_prompt, x_sample, cache_k_diff, cache_v_diff, cache_k_moba, cache_v_moba, state_hgrn, state_rwkv, state_shift, page_table, ln_mix_0, w_in_0, lambda_q1, lambda_k1, lambda_q2, lambda_k2, subln_g, rel_bias, w_out_0, ln_ffn_0, ffn_w_gate, ffn_w_up, ffn_w_down, ln_mix_1, w_in_1, hgrn_lb, hgrn_norm_g, rwkv_mu, rwkv_w0, rwkv_w2, rwkv_a0, rwkv_a2, rwkv_g2, rwkv_k_k, rwkv_k_a, rwkv_r_k, rwkv_ln_w, rwkv_ln_b, w_out_1, ln_ffn_1, moe_router, moe_w_gate, moe_w_up, moe_w_down, ln_final):
    bp, tt, d = x_prompt.shape
    db = x_sample.shape[0]
    assert x_sample.shape[1] == 1 and d == D_MODEL
    n = bp * tt

    lam_vecs = jnp.stack([lambda_q1, lambda_k1, lambda_q2, lambda_k2]).astype(F32)
    bias = _bias_tiles(rel_bias, _prompt_bias_buckets())
    last_page_rel = PAGE_SIZE - np.arange(PAGE_SIZE)
    bias_last = _bias_tiles(rel_bias, np.broadcast_to(_t5_bucket_np(last_page_rel), (1, 8, PAGE_SIZE)).astype(np.int32))
    w1, mu1 = _odd_weight_layout(w_in_1, rwkv_mu)
    par = _rwkv_param_rows(rwkv_w0, rwkv_a0, rwkv_k_k, rwkv_k_a, rwkv_r_k, rwkv_ln_w, rwkv_ln_b)
    lw = _rwkv_lora_weights(rwkv_w2, rwkv_a2, rwkv_g2)
    even_splits = (A_QK, A_QK, A_V, B_W, B_W, B_W)

    xp = x_prompt.reshape(n, d)
    qa, ka_t, va, qb, kb, vb, va4, kb4, vb4 = _even_in_proj(xp, ln_mix_0, w_in_0, tt, TM)
    seq = lambda a: a.reshape(bp, tt, a.shape[-1])
    oa = _diff_attn_prompt(rel_bias, lam_vecs, seq(qa), ka_t, seq(va), bias, subln_g)
    ob = _moba_prompt(rel_bias, seq(qb), seq(kb), seq(vb), bias)
    h = _matmul_residual(xp, oa.reshape(n, A_V), ob.reshape(n, B_W), w_out_0, TM)
    h = _ffn(h, ln_ffn_0, ffn_w_gate, ffn_w_up, ffn_w_down, TM, FFN_TF)
    cq, cf, ci, cg, r, k, v, lo, shift_p = _odd_in_proj(h, ln_mix_1, w1, mu1, TM, seq_len=tt)
    o_c, hgrn_p = _hgrn_prompt(seq(cq), seq(cf), seq(ci), seq(cg), hgrn_lb, hgrn_norm_g)
    wl, k2, kk, kka, g, bonus = _rwkv_prep(r, k, v, lo, par, lw, TM)
    o_d, h_state = _rwkv_prompt(seq(r), seq(wl), seq(k2), seq(v), seq(kk), seq(kka), seq(g), seq(bonus), par)
    rwkv_p = h_state.reshape(bp, DD, HD, DD).transpose(0, 2, 1, 3)
    h = _matmul_residual(h, o_c.reshape(n, C_V), o_d.reshape(n, D_W), w_out_1, TM)
    y_prompt = _moe_routed_final(h, ln_ffn_1, moe_router, moe_w_gate, moe_w_up, moe_w_down, ln_final, MOE_TF)

    xs = x_sample.reshape(db, d)
    qa_s, ka_s, va_s, qb_s, kb_s, vb_s = _norm_matmul(xs, ln_mix_0, w_in_0, even_splits, (F32,) * 6, TM)
    one = lambda a: a.reshape(db, 1, a.shape[-1])
    ckd_t = cache_k_diff.transpose(0, 2, 3, 4, 1)
    rows = lambda c: c.reshape(c.shape[0], PAGE_SIZE * c.shape[2], c.shape[3])
    cvd, ckm, cvm = rows(cache_v_diff), rows(cache_k_moba), rows(cache_v_moba)
    oa_s, page_sums = _decode_diff(page_table, rel_bias, lam_vecs, one(qa_s), one(ka_s), one(va_s),
                                   ckd_t, cvd, ckm, bias_last, subln_g)
    sel = _decode_gate(page_sums, one(qb_s))
    ob_s = _decode_moba(page_table, sel, rel_bias, one(qb_s), one(kb_s), one(vb_s), ckm, cvm, bias_last)
    hs = _matmul_residual(xs, oa_s.reshape(db, A_V), ob_s.reshape(db, B_W), w_out_0, TM)
    hs = _ffn(hs, ln_ffn_0, ffn_w_gate, ffn_w_up, ffn_w_down, TM, FFN_TF)
    cq, cf, ci, cg, r, k, v, lo, shift_s = _odd_in_proj(hs, ln_mix_1, w1, mu1, TM, x_prev=state_shift)
    wl, k2, kk, kka, g, bonus = _rwkv_prep(r, k, v, lo, par, lw, TM)
    rows = [one(a) for a in (cq, cf, ci, cg, r, wl, k2, v, kk, kka, g, bonus)]
    o_cd, hgrn_s, rwkv_s = _odd_step(rows, state_hgrn, state_rwkv, hgrn_lb, hgrn_norm_g, par)
    o_cd = o_cd.reshape(db, C_V + D_W)
    hs = _matmul_residual(hs, o_cd[:, :C_V], o_cd[:, C_V:], w_out_1, TM)
    y_sample = _moe_final(hs, ln_ffn_1, moe_router, moe_w_gate, moe_w_up, moe_w_down, ln_final, TM, MOE_TF)

    return (y_prompt.reshape(bp, tt, d), y_sample.reshape(db, 1, d),
            ka_t.transpose(0, 4, 1, 2, 3), va4, kb4, vb4,
            hgrn_p, rwkv_p, shift_p.reshape(bp, d),
            ka_s.reshape(db, 1, HA, 2, DA), va_s.reshape(db, 1, HA, DVA),
            kb_s.reshape(db, 1, HB, DB), vb_s.reshape(db, 1, HB, DB),
            hgrn_s, rwkv_s, shift_s)
```

```python
import functools
import math

import jax
import jax.numpy as jnp
import numpy as np
from jax import lax
from jax.experimental import pallas as pl
from jax.experimental.pallas import tpu as pltpu

F32 = jnp.float32
BF16 = jnp.bfloat16
MXU_DTYPE = jnp.bfloat16

D_MODEL = 1024
PAGE_SIZE = 128
HA, DA = 4, 64
DVA = 2 * DA
HB, DB = 4, 128
MOBA_BLOCK = 256
MOBA_TOPK = 3
N_BUCKETS = 32
MAX_DISTANCE = 128
HC, DKC, DVC = 4, 128, 128
HD, DD = 8, 64
D_DECAY_LORA, D_AAA_LORA, D_GATE_LORA = 32, 32, 96
RWKV_LN_EPS = 64e-5
N_EXPERTS = 8
EPS = 1e-6

A_QK = HA * 2 * DA
A_V = HA * DVA
B_W = HB * DB
C_K = HC * DKC
C_V = HC * DVC
HGRN_IN = 2 * C_K + 2 * C_V
D_W = HD * DD
RWKV_IN = 3 * D_W + D_DECAY_LORA + D_AAA_LORA + D_GATE_LORA

NEG = -1e30
VMEM_LIMIT_BYTES = 56 * 1024 * 1024
ATTN_TILE = 256


def _params(*sem):
    return pltpu.CompilerParams(dimension_semantics=sem, vmem_limit_bytes=VMEM_LIMIT_BYTES)


def _mm(a, b):
    return jnp.dot(a.astype(MXU_DTYPE), b.astype(MXU_DTYPE), preferred_element_type=F32)


def _mm_nt(a, b):
    return lax.dot_general(a.astype(MXU_DTYPE), b.astype(MXU_DTYPE), (((1,), (1,)), ((), ())),
                           preferred_element_type=F32)


def _rms(x, g):
    return x * lax.rsqrt(jnp.mean(x * x, axis=-1, keepdims=True) + EPS) * g


def _norm_matmul_kernel(x_ref, g_ref, w_ref, *out_refs, splits):
    y = _rms(x_ref[...], g_ref[...]).astype(MXU_DTYPE)
    off = 0
    for o_ref, width in zip(out_refs, splits):
        o_ref[...] = jnp.dot(y, w_ref[:, off:off + width], preferred_element_type=F32).astype(o_ref.dtype)
        off += width


def _norm_matmul(x, g, w, splits, dtypes, tm):
    n, d = x.shape
    tm = min(tm, n)
    assert n % tm == 0 and sum(splits) == w.shape[1]
    return pl.pallas_call(
        functools.partial(_norm_matmul_kernel, splits=tuple(splits)),
        grid=(n // tm,),
        in_specs=[pl.BlockSpec((tm, d), lambda i: (i, 0)),
                  pl.BlockSpec((1, d), lambda i: (0, 0)),
                  pl.BlockSpec(w.shape, lambda i: (0, 0))],
        out_specs=[pl.BlockSpec((tm, s), lambda i: (i, 0)) for s in splits],
        out_shape=[jax.ShapeDtypeStruct((n, s), dt) for s, dt in zip(splits, dtypes)],
        compiler_params=_params("parallel"),
        name="norm_matmul",
    )(x, g.reshape(1, d), w.astype(MXU_DTYPE))


def _even_in_proj_kernel(x_ref, g_ref, w_ref, wkt_ref, qa_ref, kat_ref, va_ref, qb_ref, kb_ref, vb_ref,
                         va4_ref, kb4_ref, vb4_ref):
    y = _rms(x_ref[...], g_ref[...]).astype(MXU_DTYPE)
    tm = y.shape[0]

    def proj(j):
        return jnp.dot(y, w_ref[:, j * A_QK:(j + 1) * A_QK], preferred_element_type=F32)

    qa_ref[...] = proj(0).astype(qa_ref.dtype)
    kat_ref[...] = _mm_nt(wkt_ref[...], y).reshape(HA, 2, DA, tm)
    qb_ref[...] = proj(3)
    for j, flat_ref, heads_ref in ((2, va_ref, va4_ref), (4, kb_ref, kb4_ref), (5, vb_ref, vb4_ref)):
        z = proj(j)
        flat_ref[...] = z
        for h in range(HA):
            heads_ref[:, h, :] = z[:, h * DVA:(h + 1) * DVA]


def _even_in_proj(x, g, w, seq_len, tm):
    n, d = x.shape
    assert n % seq_len == 0 and seq_len % tm == 0 and A_QK == A_V == B_W and HA == HB and DVA == DB
    nb, tps = n // seq_len, seq_len // tm
    flat = lambda dt: (jax.ShapeDtypeStruct((n, A_V), dt), pl.BlockSpec((tm, A_V), lambda i: (i, 0)))
    heads = (jax.ShapeDtypeStruct((nb, seq_len, HA, DVA), F32),
             pl.BlockSpec((None, tm, HA, DVA), lambda i: (i // tps, i % tps, 0, 0)))
    kat = (jax.ShapeDtypeStruct((nb, HA, 2, DA, seq_len), F32),
           pl.BlockSpec((None, HA, 2, DA, tm), lambda i: (i // tps, 0, 0, 0, i % tps)))
    outs = [flat(BF16), kat, flat(F32), flat(F32), flat(F32), flat(F32), heads, heads, heads]
    wk_t = w[:, A_QK:2 * A_QK].T.astype(MXU_DTYPE)
    return pl.pallas_call(
        _even_in_proj_kernel,
        grid=(n // tm,),
        in_specs=[pl.BlockSpec((tm, d), lambda i: (i, 0)),
                  pl.BlockSpec((1, d), lambda i: (0, 0)),
                  pl.BlockSpec(w.shape, lambda i: (0, 0)),
                  pl.BlockSpec(wk_t.shape, lambda i: (0, 0))],
        out_specs=[o[1] for o in outs],
        out_shape=[o[0] for o in outs],
        compiler_params=_params("parallel"),
        name="even_in_proj",
    )(x, g.reshape(1, d), w.astype(MXU_DTYPE), wk_t)


def _t5_bucket_np(rel):
    max_exact = N_BUCKETS // 2
    n = np.maximum(rel, 0)
    nf = np.maximum(n, max_exact).astype(np.float64)
    v = np.log(nf / max_exact) / math.log(MAX_DISTANCE / max_exact) * (N_BUCKETS - max_exact)
    frac = np.abs(v - np.round(v))
    assert np.all((frac > 1e-6) | (n <= max_exact) | (v >= N_BUCKETS - max_exact - 1e-6))
    large = max_exact + np.floor(v + 1e-9).astype(np.int64)
    return np.where(n < max_exact, n, np.minimum(large, N_BUCKETS - 1)).astype(np.int32)


def _bias_tiles_kernel(tab_ref, bk_ref, o_ref):
    h = pl.program_id(0)
    bk = bk_ref[...]
    acc = jnp.zeros(bk.shape, F32)
    for i in range(N_BUCKETS):
        acc = jnp.where(bk == i, tab_ref[i, h], acc)
    o_ref[0] = acc


def _bias_tiles(rel_bias, buckets):
    nh = rel_bias.shape[1]
    return pl.pallas_call(
        _bias_tiles_kernel,
        grid=(nh,),
        in_specs=[pl.BlockSpec(memory_space=pltpu.SMEM),
                  pl.BlockSpec(buckets.shape, lambda h: (0, 0, 0))],
        out_specs=pl.BlockSpec((1,) + buckets.shape, lambda h: (h, 0, 0, 0)),
        out_shape=jax.ShapeDtypeStruct((nh,) + buckets.shape, F32),
        compiler_params=_params("arbitrary"),
        name="bias_tiles",
    )(rel_bias, jnp.asarray(buckets))


def _softmax_pv(s, vb):
    p = jnp.exp(s - jnp.max(s, axis=-1, keepdims=True))
    return _mm(p, vb) / jnp.sum(p, axis=-1, keepdims=True)


def _causal(s):
    return jnp.where(_iota2(s.shape, 1) <= _iota2(s.shape, 0), s, NEG)


def _per_query_tile(qi, nq, compute):
    for qs in range(nq):
        pl.when(qi == qs)(functools.partial(compute, qs))


LAM_INIT_0 = 0.8 - 0.6 * math.exp(-0.3 * 0)


def _diff_lambda(lv_ref):
    s1 = jnp.sum(lv_ref[0:1, :] * lv_ref[1:2, :], axis=-1, keepdims=True)
    s2 = jnp.sum(lv_ref[2:3, :] * lv_ref[3:4, :], axis=-1, keepdims=True)
    return jnp.exp(s1) - jnp.exp(s2) + LAM_INIT_0


def _diff_attn_kernel(tab_ref, lv_ref, q_ref, k_ref, v_ref, bias_ref, g_ref, o_ref):
    h = pl.program_id(1)
    qi = pl.program_id(2)
    t = ATTN_TILE
    far = tab_ref[N_BUCKETS - 1, h]

    def compute(qs):
        w = (qs + 1) * t
        q = q_ref[...] * (DA ** -0.5)
        lane = _iota2(q.shape, 1)
        kt = k_ref[...].reshape(2 * DA, k_ref.shape[2])[:, 0:w].astype(MXU_DTYPE)
        vb = v_ref[0:w, :].astype(MXU_DTYPE)
        outs = []
        for c in range(2):
            qc = jnp.where((lane >= DA) == (c == 1), q, 0.0).astype(MXU_DTYPE)
            s = _mm(qc, kt)
            pieces = [_causal(s[:, qs * t:] + bias_ref[0])]
            if qs >= 1:
                pieces.insert(0, s[:, (qs - 1) * t:qs * t] + bias_ref[1])
            if qs >= 2:
                pieces.insert(0, s[:, :(qs - 1) * t] + far)
            outs.append(_softmax_pv(jnp.concatenate(pieces, axis=1), vb))
        o = outs[0] - _diff_lambda(lv_ref) * outs[1]
        o_ref[...] = (_rms(o, g_ref[...]) * (1.0 - LAM_INIT_0)).astype(o_ref.dtype)

    _per_query_tile(qi, v_ref.shape[0] // t, compute)


def _diff_attn_prompt(rel_bias, lam_vecs, q, k_t, v, bias, subln_g):
    b, tt, _ = q.shape
    t = ATTN_TILE
    assert tt % t == 0
    return pl.pallas_call(
        _diff_attn_kernel,
        grid=(b, HA, tt // t),
        in_specs=[pl.BlockSpec(memory_space=pltpu.SMEM),
                  pl.BlockSpec((4, DA), lambda bi, h, qi: (0, 0)),
                  pl.BlockSpec((None, t, DVA), lambda bi, h, qi: (bi, qi, h)),
                  pl.BlockSpec((None, None, 2, DA, tt), lambda bi, h, qi: (bi, h, 0, 0, 0)),
                  pl.BlockSpec((None, tt, DVA), lambda bi, h, qi: (bi, 0, h)),
                  pl.BlockSpec((None, 2, t, t), lambda bi, h, qi: (h, 0, 0, 0)),
                  pl.BlockSpec((1, DVA), lambda bi, h, qi: (0, 0))],
        out_specs=pl.BlockSpec((None, t, DVA), lambda bi, h, qi: (bi, qi, h)),
        out_shape=jax.ShapeDtypeStruct((b, tt, A_V), BF16),
        compiler_params=_params("parallel", "parallel", "arbitrary"),
        name="diff_attn_prompt",
    )(rel_bias, lam_vecs, q, k_t, v, bias, subln_g.reshape(1, DVA))


def _moba_select(gate_t, own):
    nb = gate_t.shape[0]
    blk = lax.broadcasted_iota(jnp.int32, gate_t.shape, 0)
    rank = jnp.zeros(gate_t.shape, F32)
    for m in range(nb):
        gm = gate_t[m:m + 1, :]
        beats = jnp.where(gm > gate_t, 1.0, jnp.where(gm == gate_t, jnp.where(m < blk, 1.0, 0.0), 0.0))
        rank = rank + jnp.where(m < own, beats, 0.0)
    return jnp.where(blk < own, jnp.where(rank < MOBA_TOPK, 1.0, 0.0), 0.0)


def _moba_kernel(tab_ref, q_ref, k_ref, v_ref, bias_ref, o_ref, means_ref, *, nb):
    h = pl.program_id(1)
    qi = pl.program_id(2)
    t = ATTN_TILE
    far = tab_ref[N_BUCKETS - 1, HA + h]

    @pl.when(qi == 0)
    def _():
        for j in range(nb):
            means_ref[j:j + 1, :] = jnp.sum(k_ref[j * t:(j + 1) * t, :], axis=0, keepdims=True) * (1.0 / MOBA_BLOCK)

    qf = q_ref[...]
    gate_t = lax.dot_general(means_ref[...], qf, (((1,), (1,)), ((), ())), precision=lax.Precision.HIGHEST,
                             preferred_element_type=F32)
    sel = _moba_select(gate_t, qi).T
    q = qf.astype(MXU_DTYPE)

    def compute(qs):
        w = (qs + 1) * t
        s = _mm_nt(q, k_ref[0:w, :]) * (DB ** -0.5)
        pieces = []
        for kj in range(qs):
            bias = bias_ref[1] if kj == qs - 1 else far
            pieces.append(jnp.where(sel[:, kj:kj + 1] > 0.5, s[:, kj * t:(kj + 1) * t] + bias, NEG))
        pieces.append(_causal(s[:, qs * t:] + bias_ref[0]))
        o_ref[...] = _softmax_pv(jnp.concatenate(pieces, axis=1), v_ref[0:w, :]).astype(o_ref.dtype)

    _per_query_tile(qi, nb, compute)


def _moba_prompt(rel_bias, q, k, v, bias):
    b, tt, _ = q.shape
    t = ATTN_TILE
    assert tt % t == 0 and t == MOBA_BLOCK
    nb = tt // t
    return pl.pallas_call(
        functools.partial(_moba_kernel, nb=nb),
        grid=(b, HB, nb),
        in_specs=[pl.BlockSpec(memory_space=pltpu.SMEM),
                  pl.BlockSpec((None, t, DB), lambda bi, h, qi: (bi, qi, h)),
                  pl.BlockSpec((None, tt, DB), lambda bi, h, qi: (bi, 0, h)),
                  pl.BlockSpec((None, tt, DB), lambda bi, h, qi: (bi, 0, h)),
                  pl.BlockSpec((None, 2, t, t), lambda bi, h, qi: (HA + h, 0, 0, 0))],
        out_specs=pl.BlockSpec((None, t, DB), lambda bi, h, qi: (bi, qi, h)),
        out_shape=jax.ShapeDtypeStruct((b, tt, B_W), BF16),
        scratch_shapes=[pltpu.VMEM((nb, DB), F32)],
        compiler_params=_params("parallel", "parallel", "arbitrary"),
        name="moba_prompt",
    )(rel_bias, q, k, v, bias)


def _prompt_bias_buckets():
    t = ATTN_TILE
    r = np.arange(t)[:, None]
    c = np.arange(t)[None, :]
    return np.stack([_t5_bucket_np(r - c), _t5_bucket_np(t + r - c)]).astype(np.int32)


def _matmul_residual_kernel(res_ref, a_ref, b_ref, w_ref, o_ref):
    ka = a_ref.shape[1]
    o_ref[...] = res_ref[...] + _mm(a_ref[...], w_ref[:ka, :]) + _mm(b_ref[...], w_ref[ka:, :])


def _matmul_residual(res, a, b, w, tm):
    n, d = res.shape
    tm = min(tm, n)
    assert n % tm == 0
    return pl.pallas_call(
        _matmul_residual_kernel,
        grid=(n // tm,),
        in_specs=[pl.BlockSpec((tm, d), lambda i: (i, 0)),
                  pl.BlockSpec((tm, a.shape[1]), lambda i: (i, 0)),
                  pl.BlockSpec((tm, b.shape[1]), lambda i: (i, 0)),
                  pl.BlockSpec(w.shape, lambda i: (0, 0))],
        out_specs=pl.BlockSpec((tm, d), lambda i: (i, 0)),
        out_shape=jax.ShapeDtypeStruct((n, d), F32),
        compiler_params=_params("parallel"),
        name="matmul_residual",
    )(res, a, b, w.astype(MXU_DTYPE))


def _silu(x):
    return x * jax.nn.sigmoid(x)


def _ffn_kernel(x_ref, a_ref, b_ref, wo_ref, g_ref, wg_ref, wu_ref, wd_ref, o_ref, h_ref, xn_ref, acc_ref):
    f = pl.program_id(1)

    @pl.when(f == 0)
    def _():
        ka = a_ref.shape[1]
        h = x_ref[...] + _mm(a_ref[...], wo_ref[:ka, :]) + _mm(b_ref[...], wo_ref[ka:, :])
        h_ref[...] = h
        xn_ref[...] = _rms(h, g_ref[...]).astype(xn_ref.dtype)
        acc_ref[...] = jnp.zeros(acc_ref.shape, F32)

    xn = xn_ref[...]
    a = _silu(_mm(xn, wg_ref[...])) * _mm(xn, wu_ref[...])
    acc_ref[...] += _mm(a, wd_ref[...])

    @pl.when(f == pl.num_programs(1) - 1)
    def _():
        o_ref[...] = h_ref[...] + acc_ref[...]


def _mixer_out_ffn(x, a, b, w_out, g, wg, wu, wd, tm, tf):
    n, d = x.shape
    ff = wg.shape[1]
    tm = min(tm, n)
    assert n % tm == 0 and ff % tf == 0
    return pl.pallas_call(
        _ffn_kernel,
        grid=(n // tm, ff // tf),
        in_specs=[pl.BlockSpec((tm, d), lambda i, f: (i, 0)),
                  pl.BlockSpec((tm, a.shape[1]), lambda i, f: (i, 0)),
                  pl.BlockSpec((tm, b.shape[1]), lambda i, f: (i, 0)),
                  pl.BlockSpec(w_out.shape, lambda i, f: (0, 0)),
                  pl.BlockSpec((1, d), lambda i, f: (0, 0)),
                  pl.BlockSpec((d, tf), lambda i, f: (0, f)),
                  pl.BlockSpec((d, tf), lambda i, f: (0, f)),
                  pl.BlockSpec((tf, d), lambda i, f: (f, 0))],
        out_specs=pl.BlockSpec((tm, d), lambda i, f: (i, 0)),
        out_shape=jax.ShapeDtypeStruct((n, d), F32),
        scratch_shapes=[pltpu.VMEM((tm, d), F32), pltpu.VMEM((tm, d), MXU_DTYPE), pltpu.VMEM((tm, d), F32)],
        compiler_params=_params("parallel", "arbitrary"),
        name="ffn",
    )(x, a, b, w_out.astype(MXU_DTYPE), g.reshape(1, d), wg.astype(MXU_DTYPE), wu.astype(MXU_DTYPE),
      wd.astype(MXU_DTYPE))


def _router_gates(logits):
    ne = logits.shape[1]
    lane = lax.broadcasted_iota(jnp.int32, logits.shape, 1)
    m1 = jnp.max(logits, axis=-1, keepdims=True)
    i1 = jnp.min(jnp.where(logits == m1, lane, ne), axis=-1, keepdims=True)
    rest = jnp.where(lane == i1, -jnp.inf, logits)
    m2 = jnp.max(rest, axis=-1, keepdims=True)
    i2 = jnp.min(jnp.where(rest == m2, lane, ne), axis=-1, keepdims=True)
    e2 = jnp.exp(m2 - m1)
    den = 1.0 + e2
    return jnp.where(lane == i1, 1.0 / den, 0.0) + jnp.where(lane == i2, e2 / den, 0.0)


def _moe_dense_kernel(x_ref, g_ref, wr_ref, wg_ref, wu_ref, wd_ref, gf_ref, o_ref, xn_ref, gate_ref, acc_ref):
    e = pl.program_id(1)
    f = pl.program_id(2)

    @pl.when((e == 0) & (f == 0))
    def _():
        xn = _rms(x_ref[...], g_ref[...])
        xn_ref[...] = xn.astype(xn_ref.dtype)
        logits = jnp.dot(xn, wr_ref[...], precision=lax.Precision.HIGHEST, preferred_element_type=F32)
        gate_ref[...] = _router_gates(logits)
        acc_ref[...] = jnp.zeros(acc_ref.shape, F32)

    xn = xn_ref[...]
    a = _silu(_mm(xn, wg_ref[...])) * _mm(xn, wu_ref[...])
    lane = lax.broadcasted_iota(jnp.int32, gate_ref.shape, 1)
    gate_e = jnp.sum(jnp.where(lane == e, gate_ref[...], 0.0), axis=-1, keepdims=True)
    acc_ref[...] += _mm(a, wd_ref[...]) * gate_e

    @pl.when((e == pl.num_programs(1) - 1) & (f == pl.num_programs(2) - 1))
    def _():
        o_ref[...] = _rms(x_ref[...] + acc_ref[...], gf_ref[...])


def _moe_final(x, g, w_router, wg, wu, wd, g_final, tm, tf):
    n, d = x.shape
    ne, _, ff = wg.shape
    tm = min(tm, n)
    assert n % tm == 0 and ff % tf == 0
    return pl.pallas_call(
        _moe_dense_kernel,
        grid=(n // tm, ne, ff // tf),
        in_specs=[pl.BlockSpec((tm, d), lambda i, e, f: (i, 0)),
                  pl.BlockSpec((1, d), lambda i, e, f: (0, 0)),
                  pl.BlockSpec((d, ne), lambda i, e, f: (0, 0)),
                  pl.BlockSpec((None, d, tf), lambda i, e, f: (e, 0, f)),
                  pl.BlockSpec((None, d, tf), lambda i, e, f: (e, 0, f)),
                  pl.BlockSpec((None, tf, d), lambda i, e, f: (e, f, 0)),
                  pl.BlockSpec((1, d), lambda i, e, f: (0, 0))],
        out_specs=pl.BlockSpec((tm, d), lambda i, e, f: (i, 0)),
        out_shape=jax.ShapeDtypeStruct((n, d), F32),
        scratch_shapes=[pltpu.VMEM((tm, d), MXU_DTYPE), pltpu.VMEM((tm, ne), F32), pltpu.VMEM((tm, d), F32)],
        compiler_params=_params("parallel", "arbitrary", "arbitrary"),
        name="moe_final",
    )(x, g.reshape(1, d), w_router, wg.astype(MXU_DTYPE), wu.astype(MXU_DTYPE), wd.astype(MXU_DTYPE),
      g_final.reshape(1, d))


MOE_TM = 512
META_ROWS = 8


def _moe_route_kernel(x_ref, g_ref, wrt_ref, xn_ref, mi_ref, mw_ref, cnt_ref, carry_ref):
    i = pl.program_id(0)
    tm = x_ref.shape[0]
    ne = wrt_ref.shape[0]

    @pl.when(i == 0)
    def _():
        carry_ref[...] = jnp.zeros(carry_ref.shape, F32)

    xn = _rms(x_ref[...], g_ref[...])
    xn_ref[...] = xn
    logits = lax.dot_general(wrt_ref[...], xn, (((1,), (1,)), ((), ())), precision=lax.Precision.HIGHEST,
                             preferred_element_type=F32)
    sub = _iota2(logits.shape, 0)
    m1 = jnp.max(logits, axis=0, keepdims=True)
    i1 = jnp.min(jnp.where(logits == m1, sub, ne), axis=0, keepdims=True)
    rest = jnp.where(sub == i1, -jnp.inf, logits)
    m2 = jnp.max(rest, axis=0, keepdims=True)
    i2 = jnp.min(jnp.where(rest == m2, sub, ne), axis=0, keepdims=True)
    e2 = jnp.exp(m2 - m1)
    den = 1.0 + e2
    onehot = jnp.where(sub == i1, 1.0, jnp.where(sub == i2, 1.0, 0.0))
    before = jnp.where(_iota2((tm, tm), 0) < _iota2((tm, tm), 1), 1.0, 0.0)
    rank = jnp.dot(onehot.astype(BF16), before.astype(BF16), preferred_element_type=F32) + carry_ref[:, 0:1]
    r1 = jnp.sum(jnp.where(sub == i1, rank, 0.0), axis=0, keepdims=True)
    r2 = jnp.sum(jnp.where(sub == i2, rank, 0.0), axis=0, keepdims=True)
    zi = jnp.zeros((META_ROWS - 4, tm), jnp.int32)
    mi_ref[...] = jnp.concatenate([i1, i2, r1.astype(jnp.int32), r2.astype(jnp.int32), zi], axis=0)
    mw_ref[...] = jnp.concatenate([1.0 / den, e2 / den, jnp.zeros((META_ROWS - 2, tm), F32)], axis=0)
    carry_ref[...] = carry_ref[...] + jnp.sum(onehot, axis=1, keepdims=True)
    cnt_ref[...] = carry_ref[...]


def _moe_route(x, g, w_router):
    n, d = x.shape
    ne = w_router.shape[1]
    tm = MOE_TM
    assert n % tm == 0 and ne == META_ROWS
    meta = pl.BlockSpec((None, META_ROWS, tm), lambda i: (i, 0, 0))
    return pl.pallas_call(
        _moe_route_kernel,
        grid=(n // tm,),
        in_specs=[pl.BlockSpec((tm, d), lambda i: (i, 0)),
                  pl.BlockSpec((1, d), lambda i: (0, 0)),
                  pl.BlockSpec((ne, d), lambda i: (0, 0))],
        out_specs=[pl.BlockSpec((tm, d), lambda i: (i, 0)), meta, meta,
                   pl.BlockSpec((ne, 128), lambda i: (0, 0))],
        out_shape=[jax.ShapeDtypeStruct((n, d), F32),
                   jax.ShapeDtypeStruct((n // tm, META_ROWS, tm), jnp.int32),
                   jax.ShapeDtypeStruct((n // tm, META_ROWS, tm), F32),
                   jax.ShapeDtypeStruct((ne, 128), F32)],
        scratch_shapes=[pltpu.VMEM((ne, 128), F32)],
        compiler_params=_params("arbitrary"),
        name="moe_route",
    )(x, g.reshape(1, d), w_router.T)


def _row_copy(src, src_row, dst, dst_row, sem):
    return pltpu.make_async_copy(src.at[pl.ds(src_row, 1), :], dst.at[pl.ds(dst_row, 1), :], sem)


def _moe_slots_kernel(off_ref, mi_ref, sl_ref):
    rows = []
    for slot in range(2):
        e = mi_ref[slot:slot + 1, :]
        base = jnp.zeros(e.shape, jnp.int32)
        for x in range(META_ROWS):
            base = jnp.where(e == x, off_ref[x], base)
        rows.append(base + mi_ref[2 + slot:3 + slot, :])
    sl_ref[...] = jnp.concatenate(rows + [jnp.zeros((META_ROWS - 2, mi_ref.shape[1]), jnp.int32)], axis=0)


def _moe_slots(offsets, meta_i):
    nt, _, tm = meta_i.shape
    meta = pl.BlockSpec((None, META_ROWS, tm), lambda i, off: (i, 0, 0))
    return pl.pallas_call(
        _moe_slots_kernel,
        grid_spec=pltpu.PrefetchScalarGridSpec(num_scalar_prefetch=1, grid=(nt,), in_specs=[meta], out_specs=meta),
        out_shape=jax.ShapeDtypeStruct(meta_i.shape, jnp.int32),
        compiler_params=_params("parallel"),
        name="moe_slots",
    )(offsets, meta_i)


def _moe_dispatch_kernel(sl_ref, x_ref, xs_in_ref, xs_ref, sem):
    del xs_in_ref
    tm = x_ref.shape[0]

    def start(r, carry):
        for slot in range(2):
            _row_copy(x_ref, r, xs_ref, sl_ref[slot, r], sem).start()
        return carry

    lax.fori_loop(0, tm, start, 0, unroll=8)
    for slot in range(2):
        pltpu.make_async_copy(x_ref, xs_ref.at[pl.ds(0, tm), :], sem).wait()


def _moe_dispatch(slots, xn, n_sorted):
    n, d = xn.shape
    tm = MOE_TM
    return pl.pallas_call(
        _moe_dispatch_kernel,
        grid=(n // tm,),
        in_specs=[pl.BlockSpec((None, META_ROWS, tm), lambda i: (i, 0, 0), memory_space=pltpu.SMEM),
                  pl.BlockSpec((tm, d), lambda i: (i, 0)),
                  pl.BlockSpec(memory_space=pl.ANY)],
        out_specs=pl.BlockSpec(memory_space=pl.ANY),
        scratch_shapes=[pltpu.SemaphoreType.DMA(())],
        out_shape=jax.ShapeDtypeStruct((n_sorted, d), F32),
        input_output_aliases={2: 0},
        compiler_params=_params("arbitrary"),
        name="moe_dispatch",
    )(slots, xn, jnp.zeros((n_sorted, d), F32))


def _moe_experts_kernel(te_ref, nu_ref, x_ref, wg_ref, wu_ref, wd_ref, o_ref, xb_ref, acc_ref):
    t = pl.program_id(0)
    f = pl.program_id(1)

    @pl.when(t < nu_ref[0])
    def _():
        @pl.when(f == 0)
        def _():
            xb_ref[...] = x_ref[...].astype(xb_ref.dtype)
            acc_ref[...] = jnp.zeros(acc_ref.shape, F32)

        xb = xb_ref[...]
        a = _silu(_mm(xb, wg_ref[...])) * _mm(xb, wu_ref[...])
        acc_ref[...] += _mm(a, wd_ref[...])

        @pl.when(f == pl.num_programs(1) - 1)
        def _():
            o_ref[...] = acc_ref[...]

    @pl.when((t >= nu_ref[0]) & (f == pl.num_programs(1) - 1))
    def _():
        o_ref[...] = jnp.zeros(o_ref.shape, F32)


def _moe_experts(tile_expert, n_used, xs, wg, wu, wd, tf):
    r, d = xs.shape
    ne, _, ff = wg.shape
    tm = MOE_TM
    nf = ff // tf
    assert r % tm == 0 and ff % tf == 0

    def row_tile(t, f, te, nu):
        return (jnp.minimum(t, nu[0] - 1), 0)

    def fcol(t, f, nu):
        return jnp.where(t < nu[0], f, nf - 1)

    grid_spec = pltpu.PrefetchScalarGridSpec(
        num_scalar_prefetch=2,
        grid=(r // tm, nf),
        in_specs=[pl.BlockSpec((tm, d), row_tile),
                  pl.BlockSpec((None, d, tf), lambda t, f, te, nu: (te[t], 0, fcol(t, f, nu))),
                  pl.BlockSpec((None, d, tf), lambda t, f, te, nu: (te[t], 0, fcol(t, f, nu))),
                  pl.BlockSpec((None, tf, d), lambda t, f, te, nu: (te[t], fcol(t, f, nu), 0))],
        out_specs=pl.BlockSpec((tm, d), lambda t, f, te, nu: (t, 0)),
        scratch_shapes=[pltpu.VMEM((tm, d), MXU_DTYPE), pltpu.VMEM((tm, d), F32)])
    return pl.pallas_call(
        _moe_experts_kernel,
        grid_spec=grid_spec,
        out_shape=jax.ShapeDtypeStruct((r, d), F32),
        compiler_params=_params("arbitrary", "arbitrary"),
        name="moe_experts",
    )(tile_expert, n_used, xs, wg.astype(MXU_DTYPE), wu.astype(MXU_DTYPE), wd.astype(MXU_DTYPE))


def _moe_combine_kernel(sl_ref, mw_ref, x_ref, ys_ref, gf_ref, o_ref, buf_ref, sem):
    tm = x_ref.shape[0]

    def start(r, carry):
        for slot in range(2):
            _row_copy(ys_ref, sl_ref[slot, r], buf_ref.at[slot], r, sem).start()
        return carry

    lax.fori_loop(0, tm, start, 0, unroll=8)
    for slot in range(2):
        pltpu.make_async_copy(ys_ref.at[pl.ds(0, tm), :], buf_ref.at[slot], sem).wait()
    w1 = _to_col(mw_ref[0:1, :])
    w2 = _to_col(mw_ref[1:2, :])
    o_ref[...] = _rms(x_ref[...] + (buf_ref[0] * w1 + buf_ref[1] * w2), gf_ref[...])


def _moe_combine(slots, meta_w, x, ys, g_final):
    n, d = x.shape
    tm = MOE_TM
    meta = lambda space: pl.BlockSpec((None, META_ROWS, tm), lambda i: (i, 0, 0), memory_space=space)
    return pl.pallas_call(
        _moe_combine_kernel,
        grid=(n // tm,),
        in_specs=[meta(pltpu.SMEM), meta(pltpu.VMEM),
                  pl.BlockSpec((tm, d), lambda i: (i, 0)),
                  pl.BlockSpec(memory_space=pl.ANY),
                  pl.BlockSpec((1, d), lambda i: (0, 0))],
        out_specs=pl.BlockSpec((tm, d), lambda i: (i, 0)),
        scratch_shapes=[pltpu.VMEM((2, tm, d), F32), pltpu.SemaphoreType.DMA(())],
        out_shape=jax.ShapeDtypeStruct((n, d), F32),
        compiler_params=_params("arbitrary"),
        name="moe_combine",
    )(slots, meta_w, x, ys, g_final.reshape(1, d))


def _moe_routed_final(x, g, w_router, wg, wu, wd, g_final, tf):
    n, d = x.shape
    ne = w_router.shape[1]
    tm = MOE_TM
    xn, meta_i, meta_w, counts = _moe_route(x, g, w_router)
    counts = counts[:, 0].astype(jnp.int32)
    padded = (counts + tm - 1) // tm * tm
    ends = jnp.cumsum(padded)
    offsets = ends - padded
    n_tiles = (2 * n) // tm + ne
    n_used = (ends[-1] // tm).reshape(1)
    tile_start = jnp.arange(n_tiles, dtype=jnp.int32) * tm
    tile_expert = jnp.sum(tile_start[:, None] >= ends[None, :], axis=1).astype(jnp.int32)
    tile_expert = jnp.minimum(tile_expert, tile_expert[jnp.maximum(n_used[0] - 1, 0)])
    slots = _moe_slots(offsets, meta_i)
    xs = _moe_dispatch(slots, xn, n_tiles * tm)
    ys = _moe_experts(tile_expert, n_used, xs, wg, wu, wd, tf)
    return _moe_combine(slots, meta_w, x, ys, g_final)


ODD_RWKV_COLS = 3 * D_W + 256
ODD_SPLITS = (C_K, C_K, C_V, C_V, D_W, D_W, D_W, 256)


def _odd_weight_layout(w_in, mu):
    o = HGRN_IN
    r = slice(o, o + D_W)
    wd = slice(o + D_W, o + D_W + D_DECAY_LORA)
    k = slice(wd.stop, wd.stop + D_W)
    v = slice(k.stop, k.stop + D_W)
    rest = slice(v.stop, v.stop + D_AAA_LORA + D_GATE_LORA)
    pad = 256 - (D_DECAY_LORA + D_AAA_LORA + D_GATE_LORA)
    w = jnp.concatenate([w_in[:, :o], w_in[:, r], w_in[:, k], w_in[:, v], w_in[:, wd], w_in[:, rest],
                         jnp.zeros((w_in.shape[0], pad), w_in.dtype)], axis=1)
    ro = lambda s: slice(s.start - o, s.stop - o)
    m = jnp.concatenate([mu[ro(r)], mu[ro(k)], mu[ro(v)], mu[ro(wd)], mu[ro(rest)], jnp.zeros((pad,), mu.dtype)])
    return w, m


def _odd_in_proj_kernel(x_ref, g_ref, w_ref, mu_ref, *refs, shift, tiles_per_seq):
    if shift:
        (cq, cf, ci, cg, r, k, v, lo, xs_ref, carry_ref) = refs
        xn = _rms(x_ref[...], g_ref[...])
        xs_ref[...] = xn[xn.shape[0] - 1:, :]
    else:
        (xp_ref, cq, cf, ci, cg, r, k, v, lo, xs_ref) = refs
        xn = _rms(x_ref[...], g_ref[...])
        xs_ref[...] = xn
    y = xn.astype(MXU_DTYPE)
    off = 0
    for o_ref in (cq, cf, ci, cg):
        o_ref[...] = jnp.dot(y, w_ref[:, off:off + C_K], preferred_element_type=F32)
        off += C_K
    z = jnp.dot(y, w_ref[:, off:], preferred_element_type=F32)
    if shift:
        i = pl.program_id(0)

        @pl.when(i % tiles_per_seq == 0)
        def _():
            carry_ref[...] = jnp.zeros(carry_ref.shape, F32)

        row = lax.broadcasted_iota(jnp.int32, z.shape, 0)
        zp = jnp.where(row == 0, carry_ref[...], pltpu.roll(z, 1, 0))
        carry_ref[...] = z[z.shape[0] - 1:, :]
    else:
        zp = jnp.dot(xp_ref[...].astype(MXU_DTYPE), w_ref[:, off:], preferred_element_type=F32)
    z = z + (zp - z) * mu_ref[...]
    r[...] = z[:, :D_W]
    k[...] = z[:, D_W:2 * D_W]
    v[...] = z[:, 2 * D_W:3 * D_W]
    lo[...] = z[:, 3 * D_W:]


def _odd_in_proj(x, g, w, mu, tm, seq_len=None, x_prev=None):
    n, d = x.shape
    tm = min(tm, n)
    shift = x_prev is None
    assert n % tm == 0
    if shift:
        assert seq_len % tm == 0
    tiles_per_seq = seq_len // tm if shift else 1
    outs = [jax.ShapeDtypeStruct((n, s), F32) for s in ODD_SPLITS]
    out_specs = [pl.BlockSpec((tm, s), lambda i: (i, 0)) for s in ODD_SPLITS]
    in_specs = [pl.BlockSpec((tm, d), lambda i: (i, 0)),
                pl.BlockSpec((1, d), lambda i: (0, 0)),
                pl.BlockSpec(w.shape, lambda i: (0, 0)),
                pl.BlockSpec((1, ODD_RWKV_COLS), lambda i: (0, 0))]
    args = [x, g.reshape(1, d), w.astype(MXU_DTYPE), mu.reshape(1, ODD_RWKV_COLS)]
    scratch = []
    if shift:
        nseq = n // seq_len
        outs.append(jax.ShapeDtypeStruct((nseq, 1, d), F32))
        out_specs.append(pl.BlockSpec((None, 1, d), lambda i: (i // tiles_per_seq, 0, 0)))
        scratch.append(pltpu.VMEM((1, ODD_RWKV_COLS), F32))
    else:
        in_specs.append(pl.BlockSpec((tm, d), lambda i: (i, 0)))
        args.append(x_prev)
        outs.append(jax.ShapeDtypeStruct((n, d), F32))
        out_specs.append(pl.BlockSpec((tm, d), lambda i: (i, 0)))
    return pl.pallas_call(
        functools.partial(_odd_in_proj_kernel, shift=shift, tiles_per_seq=tiles_per_seq),
        grid=(n // tm,),
        in_specs=in_specs,
        out_specs=out_specs,
        out_shape=outs,
        scratch_shapes=scratch,
        compiler_params=_params("arbitrary"),
        name="odd_in_proj",
    )(*args)


def _split3(x):
    if MXU_DTYPE == F32:
        return (x,)
    x1 = x.astype(MXU_DTYPE)
    r1 = x - x1.astype(F32)
    x2 = r1.astype(MXU_DTYPE)
    x3 = (r1 - x2.astype(F32)).astype(MXU_DTYPE)
    return (x1, x2, x3)


def _mm_exact_lhs(ones, x):
    o = ones.astype(MXU_DTYPE)
    return sum(jnp.dot(o, p, preferred_element_type=F32) for p in _split3(x))


def _mm_exact_rhs(x, ones):
    o = ones.astype(MXU_DTYPE)
    return sum(jnp.dot(p, o, preferred_element_type=F32) for p in _split3(x))


def _iota2(shape, dim):
    return lax.broadcasted_iota(jnp.int32, shape, dim)


def _group_ones(n, width):
    return jnp.where(_iota2((n, n), 0) // width == _iota2((n, n), 1) // width, 1.0, 0.0)


HGRN_CHUNK = 32
REC_TILE = 256


def _hgrn_forget(cf, lb_ref):
    l0 = lb_ref[0:1, :]
    l1 = lb_ref[1:2, :]
    mx = jnp.maximum(l0, l1)
    e0 = jnp.exp(l0 - mx)
    e1 = jnp.exp(l1 - mx)
    w0 = e0 / (e0 + e1)
    w1 = e1 / (e0 + e1)
    lb = (w0 + w1) - w0
    return lb + (1.0 - lb) * jax.nn.sigmoid(cf)


def _hgrn_kernel(cq_ref, cf_ref, ci_ref, cg_ref, lb_ref, g_ref, o_ref, s_out_ref, s_ref):
    si = pl.program_id(1)
    t = REC_TILE
    c = HGRN_CHUNK
    nc = t // c

    @pl.when(si == 0)
    def _():
        s_ref[...] = jnp.zeros(s_ref.shape, F32)

    f = _hgrn_forget(cf_ref[...], lb_ref)
    q = _silu(cq_ref[...])
    k = 1.0 - f
    logf = jnp.log(f)
    inp = ci_ref[...]
    row = _iota2((t, t), 0)
    col = _iota2((t, t), 1)
    same = row // c == col // c
    cum_m = jnp.where(same & (col <= row), 1.0, 0.0)
    ref_m = jnp.where(same & (col % c <= c // 2), 1.0, 0.0)
    last_m = jnp.where(same, 1.0, 0.0)
    b = _mm_exact_lhs(cum_m, logf)
    b_ref = _mm_exact_lhs(ref_m, logf)
    b_last = _mm_exact_lhs(last_m, logf)
    sel8 = jnp.where(_iota2((nc, t), 0) == _iota2((nc, t), 1) // c, 1.0, 0.0)
    dec_t = jnp.exp(_mm_exact_lhs(sel8, logf)).T
    qs = (q * jnp.exp(b - b_ref)).astype(MXU_DTYPE)
    ks = (k * jnp.exp(b_ref - b)).astype(MXU_DTYPE)
    q_inter = (q * jnp.exp(b)).astype(MXU_DTYPE)
    k_state = (k * jnp.exp(b_last - b)).astype(MXU_DTYPE)
    inp_m = inp.astype(MXU_DTYPE)
    causal = same & (col <= row)
    outs = []
    for h in range(HC):
        cs = slice(h * DKC, (h + 1) * DKC)
        vs = slice(h * DVC, (h + 1) * DVC)
        scores = jnp.where(causal, _mm_nt(qs[:, cs], ks[:, cs]), 0.0)
        o_h = _mm(scores, inp_m[:, vs])
        state = s_ref[h]
        inter = []
        for j in range(nc):
            rs = slice(j * c, (j + 1) * c)
            inter.append(_mm(q_inter[rs, cs], state))
            upd = lax.dot_general(k_state[rs, cs], inp_m[rs, vs], (((0,), (0,)), ((), ())),
                                  preferred_element_type=F32)
            state = state * dec_t[cs, j:j + 1] + upd
        s_ref[h] = state
        o_h = o_h + jnp.concatenate(inter, axis=0)
        outs.append(_rms(o_h, g_ref[...]))
    o = jnp.concatenate(outs, axis=1) * _silu(cg_ref[...])
    o_ref[...] = o.astype(o_ref.dtype)

    @pl.when(si == pl.num_programs(1) - 1)
    def _():
        s_out_ref[...] = s_ref[...]


def _hgrn_prompt(cq, cf, ci, cg, hgrn_lb, norm_g):
    b, tt, _ = cq.shape
    t = REC_TILE
    assert tt % t == 0
    blk = pl.BlockSpec((None, t, C_K), lambda bi, si: (bi, si, 0))
    return pl.pallas_call(
        _hgrn_kernel,
        grid=(b, tt // t),
        in_specs=[blk, blk, blk, blk,
                  pl.BlockSpec(hgrn_lb.shape, lambda bi, si: (0, 0)),
                  pl.BlockSpec((1, DVC), lambda bi, si: (0, 0))],
        out_specs=[pl.BlockSpec((None, t, C_V), lambda bi, si: (bi, si, 0)),
                   pl.BlockSpec((None, HC, DKC, DVC), lambda bi, si: (bi, 0, 0, 0))],
        out_shape=[jax.ShapeDtypeStruct((b, tt, C_V), BF16), jax.ShapeDtypeStruct((b, HC, DKC, DVC), F32)],
        scratch_shapes=[pltpu.VMEM((HC, DKC, DVC), F32)],
        compiler_params=_params("parallel", "arbitrary"),
        name="hgrn_prompt",
    )(cq, cf, ci, cg, hgrn_lb, norm_g.reshape(1, DVC))


RWKV_CHUNK = 32
LORA_COLS = 256


def _rwkv_param_rows(w0, a0, k_k, k_a, r_k, ln_w, ln_b):
    return jnp.stack([w0, a0, k_k, k_a, r_k.reshape(-1), ln_w, ln_b, jnp.zeros_like(w0)]).astype(F32)


def _rwkv_lora_weights(w2, a2, g2):
    z = jnp.zeros((3, LORA_COLS, D_W), F32)
    z = z.at[0, :D_DECAY_LORA].set(w2)
    z = z.at[1, D_DECAY_LORA:D_DECAY_LORA + D_AAA_LORA].set(a2)
    o = D_DECAY_LORA + D_AAA_LORA
    return z.at[2, o:o + D_GATE_LORA].set(g2)


def _dot_hi(a, b):
    return jnp.dot(a, b, precision=lax.Precision.HIGHEST, preferred_element_type=F32)


def _rwkv_activations(r, k, v, lo, par_ref, lw_ref):
    w0, a0, k_k, k_a, r_k = (par_ref[i:i + 1, :] for i in range(5))
    x = -(w0 + _dot_hi(jnp.tanh(lo), lw_ref[0]))
    softplus = jnp.maximum(x, 0.0) + jnp.log(1.0 + jnp.exp(-jnp.abs(x)))
    wlog = -jnp.exp(-softplus - 0.5)
    a = jax.nn.sigmoid(a0 + _dot_hi(lo, lw_ref[1]))
    g = _dot_hi(jax.nn.sigmoid(lo), lw_ref[2])
    ones = _group_ones(D_W, DD)
    kk = k * k_k
    norm = jnp.sqrt(_mm_exact_rhs(kk * kk, ones))
    kk = kk / jnp.maximum(norm, 1e-12)
    k2 = k * (1.0 + (a - 1.0) * k_a)
    bonus = _mm_exact_rhs(r * k2 * r_k, ones) * v
    return wlog, k2, kk, kk * a, g, bonus


def _rwkv_finish(y, bonus, g, par_ref):
    ln_w = par_ref[5:6, :]
    ln_b = par_ref[6:7, :]
    ones = _group_ones(D_W, DD)
    mu = _mm_exact_rhs(y, ones) * (1.0 / DD)
    d = y - mu
    var = _mm_exact_rhs(d * d, ones) * (1.0 / DD)
    return (d * lax.rsqrt(var + RWKV_LN_EPS) * ln_w + ln_b + bonus) * g


def _rwkv_prep_kernel(r_ref, k_ref, v_ref, lo_ref, par_ref, lw_ref, wl_ref, k2_ref, kk_ref, kka_ref, g_ref, bo_ref):
    outs = _rwkv_activations(r_ref[...], k_ref[...], v_ref[...], lo_ref[...], par_ref, lw_ref)
    for o_ref, val in zip((wl_ref, k2_ref, kk_ref, kka_ref, g_ref, bo_ref), outs):
        o_ref[...] = val


def _rwkv_prep(r, k, v, lo, par, lw, tm):
    n = r.shape[0]
    tm = min(tm, n)
    assert n % tm == 0
    blk = pl.BlockSpec((tm, D_W), lambda i: (i, 0))
    return pl.pallas_call(
        _rwkv_prep_kernel,
        grid=(n // tm,),
        in_specs=[blk, blk, blk, pl.BlockSpec((tm, LORA_COLS), lambda i: (i, 0)),
                  pl.BlockSpec(par.shape, lambda i: (0, 0)),
                  pl.BlockSpec(lw.shape, lambda i: (0, 0, 0))],
        out_specs=[blk] * 6,
        out_shape=[jax.ShapeDtypeStruct((n, D_W), F32)] * 6,
        compiler_params=_params("parallel"),
        name="rwkv_prep",
    )(r, k, v, lo, par, lw)


def _pack_heads(x):
    return jnp.concatenate([x[:, h * DD:(h + 1) * DD] for h in range(HD)], axis=0)


def _unpack_heads(x):
    c = x.shape[0] // HD
    return jnp.concatenate([x[h * c:(h + 1) * c, :] for h in range(HD)], axis=1)


def _rwkv_masks(c):
    n = HD * c
    row = _iota2((n, n), 0)
    col = _iota2((n, n), 1)
    same = row // c == col // c
    keep = _iota2((n, HD * DD), 0) // c == _iota2((n, HD * DD), 1) // DD
    tri = jnp.where(_iota2((c, c), 1) <= _iota2((c, c), 0), 1.0, 0.0)
    return same & (col < row), same & (col <= row), jnp.where(row == col, 1.0, 0.0), keep, tri


def _tile_heads(x, keep):
    return jnp.where(keep, jnp.concatenate([x] * HD, axis=0), 0.0).astype(MXU_DTYPE)


def _rwkv_chunk(r, wl, k, v, al, be, ht, masks):
    c = r.shape[0]
    strict, incl, eye, keep, tri = masks
    cum = _mm_exact_lhs(tri, wl)
    e_neg = jnp.exp(-cum)
    r_hat = r * jnp.exp(cum)
    a_hat = al * jnp.exp(cum - wl)
    k_til = k * e_neg
    b_til = be * e_neg
    pr, pa, pk, pb = (_pack_heads(x).astype(MXU_DTYPE) for x in (r_hat, a_hat, k_til, b_til))
    pv_t = _pack_heads(v).T.astype(MXU_DTYPE)
    nmat = jnp.where(strict, _mm_nt(pa, pb), 0.0)
    a_k = jnp.where(strict, _mm_nt(pa, pk), 0.0)
    r_k = jnp.where(incl, _mm_nt(pr, pk), 0.0)
    r_b = jnp.where(incl, _mm_nt(pr, pb), 0.0)
    inv = eye - nmat
    pw = nmat
    steps = int(math.log2(c))
    assert 2 ** steps == c
    for _ in range(steps - 1):
        pw = _mm(pw, pw)
        inv = inv + _mm(inv, pw)
    w_t = _mm_nt(ht, _tile_heads(a_hat, keep)) + _mm_nt(pv_t, a_k)
    u_t = _mm_nt(w_t, inv)
    y_t = _mm_nt(ht, _tile_heads(r_hat, keep)) + _mm_nt(pv_t, r_k) - _mm_nt(u_t, r_b)
    gam = jnp.exp(cum[c - 1:c, :])
    ht_new = gam * (ht + _mm(pv_t, _tile_heads(k_til, keep)) - _mm(u_t, _tile_heads(b_til, keep)))
    return _unpack_heads(y_t.T), ht_new


RWKV_SEQS = 2


def _rwkv_kernel(r_ref, wl_ref, k_ref, v_ref, al_ref, be_ref, g_ref, bo_ref, par_ref, o_ref, s_out_ref, h_ref, y_ref):
    si = pl.program_id(1)
    c = RWKV_CHUNK
    nseq = r_ref.shape[0]

    @pl.when(si == 0)
    def _():
        h_ref[...] = jnp.zeros(h_ref.shape, F32)

    masks = _rwkv_masks(c)

    states = [h_ref[s] for s in range(nseq)]
    for j in range(REC_TILE // c):
        rows = slice(j * c, (j + 1) * c)
        for s in range(nseq):
            y, states[s] = _rwkv_chunk(r_ref[s, rows, :], wl_ref[s, rows, :], k_ref[s, rows, :], v_ref[s, rows, :],
                                       al_ref[s, rows, :], be_ref[s, rows, :], states[s], masks)
            y_ref[s, rows, :] = y
    for s in range(nseq):
        h_ref[s] = states[s]
    for s in range(nseq):
        o_ref[s] = _rwkv_finish(y_ref[s], bo_ref[s], g_ref[s], par_ref).astype(o_ref.dtype)

    @pl.when(si == pl.num_programs(1) - 1)
    def _():
        s_out_ref[...] = h_ref[...]


def _rwkv_prompt(r, wl, k, v, al, be, g, bonus, par):
    b, tt, _ = r.shape
    t = REC_TILE
    ns = RWKV_SEQS if b % RWKV_SEQS == 0 else 1
    assert tt % t == 0
    blk = pl.BlockSpec((ns, t, D_W), lambda bi, si: (bi, si, 0))
    return pl.pallas_call(
        _rwkv_kernel,
        grid=(b // ns, tt // t),
        in_specs=[blk] * 8 + [pl.BlockSpec(par.shape, lambda bi, si: (0, 0))],
        out_specs=[blk, pl.BlockSpec((ns, DD, HD * DD), lambda bi, si: (bi, 0, 0))],
        out_shape=[jax.ShapeDtypeStruct((b, tt, D_W), BF16), jax.ShapeDtypeStruct((b, DD, HD * DD), F32)],
        scratch_shapes=[pltpu.VMEM((ns, DD, HD * DD), F32), pltpu.VMEM((ns, t, D_W), F32)],
        compiler_params=_params("parallel", "arbitrary"),
        name="rwkv_prompt",
    )(r, wl, k, v, al, be, g, bonus, par)


def _head_rows(bl_ref, first, count, reps):
    rows = []
    for i in range(count):
        rows += [bl_ref[first + i, 0, 0:1, :]] * reps
    return jnp.concatenate(rows, axis=0) if len(rows) > 1 else rows[0]


DECODE_PAGES = 8


def _head_rows_of_page(page_ref, h, n_heads):
    return page_ref[pl.ds(h, PAGE_SIZE, stride=n_heads), :]


def _decode_diff_kernel(pt_ref, tab_ref, lv_ref, q_ref, kn_ref, vn_ref, *refs):
    pp = DECODE_PAGES
    ck, cv, cm = refs[:pp], refs[pp:2 * pp], refs[2 * pp:3 * pp]
    bl_ref, g_ref, o_ref, sums_ref, m_ref, l_ref, acc_ref, qm_ref = refs[3 * pp:]
    p = pl.program_id(1)
    last = pl.num_programs(1) - 1
    nr = 2 * HA
    rowc = _iota2((nr, 1), 0)

    def head_col(bucket):
        col = jnp.zeros((nr, 1), F32)
        for h in range(HA):
            col = jnp.where(rowc // 2 == h, tab_ref[bucket, h], col)
        return col

    @pl.when(p == 0)
    def _():
        q = q_ref[...] * (DA ** -0.5)
        keep = _iota2((nr, A_QK), 1) // DA == _iota2((nr, A_QK), 0)
        qm = jnp.where(keep, jnp.broadcast_to(q, (nr, A_QK)), 0.0)
        qm_ref[...] = qm
        m_ref[...] = jnp.sum(qm * kn_ref[...], axis=-1, keepdims=True) + head_col(0)
        l_ref[...] = jnp.ones(l_ref.shape, F32)
        for h in range(HA):
            acc_ref[h] = jnp.broadcast_to(vn_ref[:, h * DVA:(h + 1) * DVA], (nr, DVA))

    qm = qm_ref[...].astype(MXU_DTYPE)
    s = jnp.concatenate([jnp.dot(qm, ck[j][...].reshape(A_QK, PAGE_SIZE).astype(MXU_DTYPE),
                                 preferred_element_type=F32) for j in range(pp)], axis=1)
    far = head_col(N_BUCKETS - 1)
    near = jnp.concatenate([jnp.broadcast_to(far, (nr, (pp - 1) * PAGE_SIZE)), _head_rows(bl_ref, 0, HA, 2)], axis=1)
    s = s + jnp.where(p == last, near, far)
    m_prev = m_ref[...]
    m_new = jnp.maximum(m_prev, jnp.max(s, axis=-1, keepdims=True))
    alpha = jnp.exp(m_prev - m_new)
    pe = jnp.exp(s - m_new)
    l_ref[...] = alpha * l_ref[...] + jnp.sum(pe, axis=-1, keepdims=True)
    for h in range(HA):
        vh = jnp.concatenate([_head_rows_of_page(cv[j], h, HA) for j in range(pp)], axis=0)
        acc_ref[h] = alpha * acc_ref[h] + _mm(pe, vh)
    m_ref[...] = m_new
    for j in range(pp):
        part = jnp.sum(cm[j][...].reshape(PAGE_SIZE * HB // 8, 8, DB), axis=0)
        sums_ref[j] = part[0:HB] + part[HB:2 * HB]

    @pl.when(p == last)
    def _():
        lam = _diff_lambda(lv_ref)
        pieces = []
        for h in range(HA):
            outs = acc_ref[h] / l_ref[...]
            o_h = outs[2 * h:2 * h + 1, :] - lam * outs[2 * h + 1:2 * h + 2, :]
            pieces.append(_rms(o_h, g_ref[...]) * (1.0 - LAM_INIT_0))
        o_ref[...] = jnp.concatenate(pieces, axis=1)


def _decode_diff(page_table, rel_bias, lam_vecs, q, k_new, v_new, ckd_t, cvd, ckm, bias_last, subln_g):
    db, n_pages = page_table.shape
    pp = DECODE_PAGES
    assert n_pages % pp == 0
    row = pl.BlockSpec((None, 1, A_V), lambda b, p, pt: (b, 0, 0))

    def page_spec(block, j):
        zeros = (0,) * len(block)
        return pl.BlockSpec((None,) + block, lambda b, p, pt: (pt[b * n_pages + p * pp + j],) + zeros)

    kt_pages = [page_spec((HA, 2, DA, PAGE_SIZE), j) for j in range(pp)]
    v_pages = [page_spec((PAGE_SIZE * HA, DVA), j) for j in range(pp)]
    m_pages = [page_spec((PAGE_SIZE * HB, DB), j) for j in range(pp)]
    grid_spec = pltpu.PrefetchScalarGridSpec(
        num_scalar_prefetch=1,
        grid=(db, n_pages // pp),
        in_specs=[pl.BlockSpec(memory_space=pltpu.SMEM),
                  pl.BlockSpec((4, DA), lambda b, p, pt: (0, 0)),
                  row, row, row, *kt_pages, *v_pages, *m_pages,
                  pl.BlockSpec(bias_last.shape, lambda b, p, pt: (0, 0, 0, 0)),
                  pl.BlockSpec((1, DVA), lambda b, p, pt: (0, 0))],
        out_specs=[row, pl.BlockSpec((None, pp, HB, DB), lambda b, p, pt: (b, p, 0, 0))],
        scratch_shapes=[pltpu.VMEM((2 * HA, 1), F32), pltpu.VMEM((2 * HA, 1), F32),
                        pltpu.VMEM((HA, 2 * HA, DVA), F32), pltpu.VMEM((2 * HA, A_QK), F32)])
    return pl.pallas_call(
        _decode_diff_kernel,
        grid_spec=grid_spec,
        out_shape=[jax.ShapeDtypeStruct((db, 1, A_V), F32), jax.ShapeDtypeStruct((db, n_pages, HB, DB), F32)],
        compiler_params=_params("parallel", "arbitrary"),
        name="decode_diff",
    )(page_table.reshape(-1), rel_bias, lam_vecs, q, k_new, v_new, *([ckd_t] * pp), *([cvd] * pp), *([ckm] * pp),
      bias_last, subln_g.reshape(1, DVA))


SEL_LANES = 128


def _decode_gate_kernel(sums_ref, q_ref, sel_ref):
    n_pages = sums_ref.shape[0]
    ppb = MOBA_BLOCK // PAGE_SIZE
    nb = n_pages // ppb
    pair = jnp.where(_iota2((nb, n_pages), 1) // ppb == _iota2((nb, n_pages), 0), 1.0, 0.0)
    blk = _iota2((nb, 1), 0)
    lane = _iota2((1, SEL_LANES), 1)
    out = jnp.zeros((1, SEL_LANES), jnp.int32)
    for h in range(HB):
        means = _mm_exact_lhs(pair, sums_ref[:, h, :]) * (1.0 / MOBA_BLOCK)
        gate = jnp.sum(means * q_ref[:, h * DB:(h + 1) * DB], axis=-1, keepdims=True)
        for slot in range(MOBA_TOPK):
            best = jnp.max(gate, axis=0, keepdims=True)
            idx = jnp.min(jnp.where(gate == best, blk, nb), axis=0, keepdims=True)
            out = jnp.where(lane == h * MOBA_TOPK + slot, idx, out)
            gate = jnp.where(blk == idx, -jnp.inf, gate)
    sel_ref[...] = out


def _decode_gate(sums, q):
    db, n_pages = sums.shape[:2]
    assert n_pages % (MOBA_BLOCK // PAGE_SIZE) == 0 and n_pages * PAGE_SIZE // MOBA_BLOCK >= MOBA_TOPK
    return pl.pallas_call(
        _decode_gate_kernel,
        grid=(db,),
        in_specs=[pl.BlockSpec((None, n_pages, HB, DB), lambda b: (b, 0, 0, 0)),
                  pl.BlockSpec((None, 1, B_W), lambda b: (b, 0, 0))],
        out_specs=pl.BlockSpec((None, 1, SEL_LANES), lambda b: (b, 0, 0)),
        out_shape=jax.ShapeDtypeStruct((db, 1, SEL_LANES), jnp.int32),
        compiler_params=_params("parallel"),
        name="decode_gate",
    )(sums, q)


def _decode_moba_kernel(pt_ref, sel_ref, tab_ref, q_ref, kn_ref, vn_ref, *refs, n_pages):
    ppb = MOBA_BLOCK // PAGE_SIZE
    npg = MOBA_TOPK * ppb
    ck, cv = refs[:npg], refs[npg:2 * npg]
    bl_ref, o_ref = refs[2 * npg:]
    b = pl.program_id(0)
    h = pl.program_id(1)
    scale = DB ** -0.5
    q = q_ref[...]
    q8 = jnp.broadcast_to(q, (8, DB))
    s_new = jnp.sum(q * kn_ref[...], axis=-1, keepdims=True) * scale + tab_ref[0, HA + h]
    far = tab_ref[N_BUCKETS - 1, HA + h]

    def compute(hs):
        pieces = []
        for s in range(npg):
            page = sel_ref[b * SEL_LANES + h * MOBA_TOPK + s // ppb] * ppb + s % ppb
            sc = _mm_nt(q8, _head_rows_of_page(ck[s], hs, HB))[0:1, :] * scale
            pieces.append(sc + jnp.where(page == n_pages - 1, bl_ref[HA + hs, 0, 0:1, :], far))
        sc = jnp.concatenate(pieces, axis=1)
        m = jnp.maximum(jnp.max(sc, axis=-1, keepdims=True), s_new)
        e_new = jnp.exp(s_new - m)
        pe = jnp.exp(sc - m)
        acc = e_new * vn_ref[...]
        for s in range(npg):
            pe_s = jnp.broadcast_to(pe[:, s * PAGE_SIZE:(s + 1) * PAGE_SIZE], (8, PAGE_SIZE))
            acc = acc + _mm(pe_s, _head_rows_of_page(cv[s], hs, HB))[0:1, :]
        o_ref[...] = acc / (e_new + jnp.sum(pe, axis=-1, keepdims=True))

    for hs in range(HB):
        pl.when(h == hs)(functools.partial(compute, hs))


def _decode_moba(page_table, sel, rel_bias, q, k_new, v_new, ckm, cvm, bias_last):
    db, n_pages = page_table.shape
    ppb = MOBA_BLOCK // PAGE_SIZE
    npg = MOBA_TOPK * ppb
    row = pl.BlockSpec((None, 1, DB), lambda b, h, pt, sl: (b, 0, h))

    def page_spec(s):
        def index(b, h, pt, sl):
            blk = sl[b * SEL_LANES + h * MOBA_TOPK + s // ppb]
            return (pt[b * n_pages + blk * ppb + s % ppb], 0, 0)
        return pl.BlockSpec((None, PAGE_SIZE * HB, DB), index)

    pages = [page_spec(s) for s in range(npg)]
    grid_spec = pltpu.PrefetchScalarGridSpec(
        num_scalar_prefetch=2,
        grid=(db, HB),
        in_specs=[pl.BlockSpec(memory_space=pltpu.SMEM), row, row, row, *pages, *pages,
                  pl.BlockSpec(bias_last.shape, lambda b, h, pt, sl: (0, 0, 0, 0))],
        out_specs=row)
    return pl.pallas_call(
        functools.partial(_decode_moba_kernel, n_pages=n_pages),
        grid_spec=grid_spec,
        out_shape=jax.ShapeDtypeStruct((db, 1, B_W), F32),
        compiler_params=_params("parallel", "arbitrary"),
        name="decode_moba",
    )(page_table.reshape(-1), sel.reshape(-1), rel_bias, q, k_new, v_new, *([ckm] * npg), *([cvm] * npg), bias_last)


def _to_col(row):
    n = row.shape[1]
    eye = _iota2((n, n), 0) == _iota2((n, n), 1)
    return jnp.sum(jnp.where(eye, row, 0.0), axis=1, keepdims=True)


def _to_row(col):
    n = col.shape[0]
    eye = _iota2((n, n), 0) == _iota2((n, n), 1)
    return jnp.sum(jnp.where(eye, col, 0.0), axis=0, keepdims=True)


def _odd_step_kernel(cq_ref, cf_ref, ci_ref, cg_ref, r_ref, wl_ref, k_ref, v_ref, kk_ref, kka_ref, g_ref, bo_ref,
                     sh_ref, sr_ref, lb_ref, ng_ref, par_ref, o_ref, sh_out_ref, sr_out_ref):
    f = _hgrn_forget(cf_ref[...], lb_ref)
    q = _silu(cq_ref[...])
    k = 1.0 - f
    inp = ci_ref[...]
    outs = []
    for h in range(HC):
        cs = slice(h * DKC, (h + 1) * DKC)
        vs = slice(h * DVC, (h + 1) * DVC)
        s_new = sh_ref[h] * _to_col(f[:, cs]) + _to_col(k[:, cs]) * inp[:, vs]
        sh_out_ref[h] = s_new
        outs.append(_rms(jnp.sum(s_new * _to_col(q[:, cs]), axis=0, keepdims=True), ng_ref[...]))
    o_c = jnp.concatenate(outs, axis=1) * _silu(cg_ref[...])
    r = r_ref[...]
    w = jnp.exp(wl_ref[...])
    k2 = k_ref[...]
    v = v_ref[...]
    kk = kk_ref[...]
    kka = kka_ref[...]
    ys = []
    for h in range(HD):
        cs = slice(h * DD, (h + 1) * DD)
        s_old = sr_ref[h]
        sa = jnp.sum(s_old * (-kk[:, cs]), axis=1, keepdims=True)
        s_new = s_old * w[:, cs] + sa * kka[:, cs] + _to_col(v[:, cs]) * k2[:, cs]
        sr_out_ref[h] = s_new
        ys.append(_to_row(jnp.sum(s_new * r[:, cs], axis=1, keepdims=True)))
    y = jnp.broadcast_to(jnp.concatenate(ys, axis=1), (8, D_W))
    o_d = _rwkv_finish(y, bo_ref[...], g_ref[...], par_ref)[0:1, :]
    o_ref[...] = jnp.concatenate([o_c, o_d], axis=1)


def _odd_step(rows, s_hgrn, s_rwkv, hgrn_lb, norm_g, par):
    db = s_hgrn.shape[0]
    row = pl.BlockSpec((None, 1, D_W), lambda b: (b, 0, 0))
    sh = pl.BlockSpec((None, HC, DKC, DVC), lambda b: (b, 0, 0, 0))
    sr = pl.BlockSpec((None, HD, DD, DD), lambda b: (b, 0, 0, 0))
    return pl.pallas_call(
        _odd_step_kernel,
        grid=(db,),
        in_specs=[row] * 12 + [sh, sr,
                               pl.BlockSpec(hgrn_lb.shape, lambda b: (0, 0)),
                               pl.BlockSpec((1, DVC), lambda b: (0, 0)),
                               pl.BlockSpec(par.shape, lambda b: (0, 0))],
        out_specs=[pl.BlockSpec((None, 1, C_V + D_W), lambda b: (b, 0, 0)), sh, sr],
        out_shape=[jax.ShapeDtypeStruct((db, 1, C_V + D_W), F32),
                   jax.ShapeDtypeStruct(s_hgrn.shape, F32), jax.ShapeDtypeStruct(s_rwkv.shape, F32)],
        compiler_params=_params("parallel"),
        name="odd_step",
    )(*rows, s_hgrn, s_rwkv, hgrn_lb, norm_g.reshape(1, DVC), par)


TM = 512
FFN_TF = 1408
MOE_TF = 1792


def kernel(x_prompt, x_sample, cache_k_diff, cache_v_diff, cache_k_moba, cache_v_moba, state_hgrn, state_rwkv, state_shift, page_table, ln_mix_0, w_in_0, lambda_q1, lambda_k1, lambda_q2, lambda_k2, subln_g, rel_bias, w_out_0, ln_ffn_0, ffn_w_gate, ffn_w_up, ffn_w_down, ln_mix_1, w_in_1, hgrn_lb, hgrn_norm_g, rwkv_mu, rwkv_w0, rwkv_w2, rwkv_a0, rwkv_a2, rwkv_g2, rwkv_k_k, rwkv_k_a, rwkv_r_k, rwkv_ln_w, rwkv_ln_b, w_out_1, ln_ffn_1, moe_router, moe_w_gate, moe_w_up, moe_w_down, ln_final):
    bp, tt, d = x_prompt.shape
    db = x_sample.shape[0]
    assert x_sample.shape[1] == 1 and d == D_MODEL
    n = bp * tt

    lam_vecs = jnp.stack([lambda_q1, lambda_k1, lambda_q2, lambda_k2]).astype(F32)
    bias = _bias_tiles(rel_bias, _prompt_bias_buckets())
    last_page_rel = PAGE_SIZE - np.arange(PAGE_SIZE)
    bias_last = _bias_tiles(rel_bias, np.broadcast_to(_t5_bucket_np(last_page_rel), (1, 8, PAGE_SIZE)).astype(np.int32))
    w1, mu1 = _odd_weight_layout(w_in_1, rwkv_mu)
    par = _rwkv_param_rows(rwkv_w0, rwkv_a0, rwkv_k_k, rwkv_k_a, rwkv_r_k, rwkv_ln_w, rwkv_ln_b)
    lw = _rwkv_lora_weights(rwkv_w2, rwkv_a2, rwkv_g2)
    even_splits = (A_QK, A_QK, A_V, B_W, B_W, B_W)

    xp = x_prompt.reshape(n, d)
    qa, ka_t, va, qb, kb, vb, va4, kb4, vb4 = _even_in_proj(xp, ln_mix_0, w_in_0, tt, TM)
    seq = lambda a: a.reshape(bp, tt, a.shape[-1])
    oa = _diff_attn_prompt(rel_bias, lam_vecs, seq(qa), ka_t, seq(va), bias, subln_g)
    ob = _moba_prompt(rel_bias, seq(qb), seq(kb), seq(vb), bias)
    h = _mixer_out_ffn(xp, oa.reshape(n, A_V), ob.reshape(n, B_W), w_out_0, ln_ffn_0, ffn_w_gate, ffn_w_up,
                       ffn_w_down, TM, FFN_TF)
    cq, cf, ci, cg, r, k, v, lo, shift_p = _odd_in_proj(h, ln_mix_1, w1, mu1, TM, seq_len=tt)
    o_c, hgrn_p = _hgrn_prompt(seq(cq), seq(cf), seq(ci), seq(cg), hgrn_lb, hgrn_norm_g)
    wl, k2, kk, kka, g, bonus = _rwkv_prep(r, k, v, lo, par, lw, TM)
    o_d, h_state = _rwkv_prompt(seq(r), seq(wl), seq(k2), seq(v), seq(kk), seq(kka), seq(g), seq(bonus), par)
    rwkv_p = h_state.reshape(bp, DD, HD, DD).transpose(0, 2, 1, 3)
    h = _matmul_residual(h, o_c.reshape(n, C_V), o_d.reshape(n, D_W), w_out_1, TM)
    y_prompt = _moe_routed_final(h, ln_ffn_1, moe_router, moe_w_gate, moe_w_up, moe_w_down, ln_final, MOE_TF)

    xs = x_sample.reshape(db, d)
    qa_s, ka_s, va_s, qb_s, kb_s, vb_s = _norm_matmul(xs, ln_mix_0, w_in_0, even_splits, (F32,) * 6, TM)
    one = lambda a: a.reshape(db, 1, a.shape[-1])
    ckd_t = cache_k_diff.transpose(0, 2, 3, 4, 1)
    rows = lambda c: c.reshape(c.shape[0], PAGE_SIZE * c.shape[2], c.shape[3])
    cvd, ckm, cvm = rows(cache_v_diff), rows(cache_k_moba), rows(cache_v_moba)
    oa_s, page_sums = _decode_diff(page_table, rel_bias, lam_vecs, one(qa_s), one(ka_s), one(va_s),
                                   ckd_t, cvd, ckm, bias_last, subln_g)
    sel = _decode_gate(page_sums, one(qb_s))
    ob_s = _decode_moba(page_table, sel, rel_bias, one(qb_s), one(kb_s), one(vb_s), ckm, cvm, bias_last)
    hs = _mixer_out_ffn(xs, oa_s.reshape(db, A_V), ob_s.reshape(db, B_W), w_out_0, ln_ffn_0, ffn_w_gate, ffn_w_up,
                        ffn_w_down, TM, FFN_TF)
    cq, cf, ci, cg, r, k, v, lo, shift_s = _odd_in_proj(hs, ln_mix_1, w1, mu1, TM, x_prev=state_shift)
    wl, k2, kk, kka, g, bonus = _rwkv_prep(r, k, v, lo, par, lw, TM)
    rows = [one(a) for a in (cq, cf, ci, cg, r, wl, k2, v, kk, kka, g, bonus)]
    o_cd, hgrn_s, rwkv_s = _odd_step(rows, state_hgrn, state_rwkv, hgrn_lb, hgrn_norm_g, par)
    o_cd = o_cd.reshape(db, C_V + D_W)
    hs = _matmul_residual(hs, o_cd[:, :C_V], o_cd[:, C_V:], w_out_1, TM)
    y_sample = _moe_final(hs, ln_ffn_1, moe_router, moe_w_gate, moe_w_up, moe_w_down, ln_final, TM, MOE_TF)

    return (y_prompt.reshape(bp, tt, d), y_sample.reshape(db, 1, d),
            ka_t.transpose(0, 4, 1, 2, 3), va4, kb4, vb4,
            hgrn_p, rwkv_p, shift_p.reshape(bp, d),
            ka_s.reshape(db, 1, HA, 2, DA), va_s.reshape(db, 1, HA, DVA),
            kb_s.reshape(db, 1, HB, DB), vb_s.reshape(db, 1, HB, DB),
            hgrn_s, rwkv_s, shift_s)
```
